```python
import math
import jax, jax.numpy as jnp
from jax import lax
import numpy as np

D_MODEL = 1024
BATCH = 4
SEQ = 8192
DEPTH = 4

CTX_LEN = 256
GRID_W = 64
HEAD_DIM = 64
RW_W = 3 * D_MODEL // 8
RW_HEADS = RW_W // HEAD_DIM
RW_DECAY_LORA = 64
RW_ICLR_LORA = 64
RW_GATE_LORA = 128
GD_W = 3 * D_MODEL // 8
GD_HEADS = GD_W // HEAD_DIM
GD_CONV = 3
GD_CHUNK = 64
S5_W = D_MODEL - RW_W - GD_W
S5_CH = 16
S5_GROUPS = S5_W // S5_CH
S5_STATE = 64
S5_DT_MIN = 1e-3
S5_DT_MAX = 1e-1
FFN_HIDDEN = 256 * math.ceil(8 * D_MODEL / (3 * 256))
IN_RW = 3 * RW_W + RW_DECAY_LORA + RW_ICLR_LORA + RW_GATE_LORA
IN_GD = 3 * GD_W + 4 * GD_HEADS + GD_W
IN_TOTAL = IN_RW + IN_GD + S5_W
NORM_EPS = 1e-6
LNX_EPS = 64e-5
F32 = jnp.float32

kernel_name = 'hybrid_rwkv7_gdn_s5_adaln_prefix'


def rms_norm(x, g):
    xf = x.astype(F32)
    y = xf * lax.rsqrt(jnp.mean(xf * xf, axis=-1, keepdims=True) + NORM_EPS)
    return (y * g).astype(x.dtype)


def l2_normalize(t):
    t = t.astype(F32)
    return t * lax.rsqrt(jnp.sum(t * t, axis=-1, keepdims=True) + NORM_EPS)


def centred_shift(p):
    prev = jnp.pad(p[:, :-1], ((0, 0), (1, 0), (0, 0)))
    nxt = jnp.pad(p[:, 1:], ((0, 0), (0, 1), (0, 0)))
    return 0.5 * (prev + nxt)


def depthwise_conv_centred(x, w):
    pad = w.shape[0] // 2
    return lax.conv_general_dilated(x, w[:, None, :].astype(x.dtype), window_strides=(1,),
                                    padding=[(pad, pad)], dimension_numbers=('NWC', 'WIO', 'NWC'),
                                    feature_group_count=x.shape[-1])


def raster_to_column_major(t):
    bsz, n, ch = t.shape
    rows = n // GRID_W
    return t.reshape(bsz, rows, GRID_W, ch).transpose(0, 2, 1, 3).reshape(bsz, n, ch)


def column_major_to_raster(t):
    bsz, n, ch = t.shape
    rows = n // GRID_W
    return t.reshape(bsz, GRID_W, rows, ch).transpose(0, 2, 1, 3).reshape(bsz, n, ch)


def head_group_norm(y, gain, bias):
    mu = jnp.mean(y, axis=-1, keepdims=True)
    var = jnp.mean(jnp.square(y - mu), axis=-1, keepdims=True)
    yn = (y - mu) * lax.rsqrt(var + LNX_EPS)
    bsz, n, h, d = y.shape
    return yn.reshape(bsz, n, h * d) * gain + bias


def rwkv7_scan(r, w, k, v, a, b, s0, reverse):
    def step(S, inp):
        r_t, w_t, k_t, v_t, a_t, b_t = inp
        sa = jnp.einsum('bhvk,bhk->bhv', S, a_t)
        S = S * w_t[:, :, None, :] + sa[..., None] * b_t[:, :, None, :] + v_t[..., None] * k_t[:, :, None, :]
        return S, jnp.einsum('bhvk,bhk->bhv', S, r_t)
    xs = tuple(jnp.moveaxis(t.astype(F32), 1, 0) for t in (r, w, k, v, a, b))
    s_final, ys = lax.scan(step, s0, xs, reverse=reverse)
    return jnp.moveaxis(ys, 0, 1), s_final


def rwkv7_mixer(z, P, s0):
    bsz, n, _ = z.shape
    z = z + (centred_shift(z) - z) * P['rw_mu']
    r, k, v, wd, ad, gd = jnp.split(z, [RW_W, 2 * RW_W, 3 * RW_W, 3 * RW_W + RW_DECAY_LORA,
                                        3 * RW_W + RW_DECAY_LORA + RW_ICLR_LORA], axis=-1)
    heads = lambda t: t.reshape(bsz, n, RW_HEADS, HEAD_DIM)
    gate = jax.nn.sigmoid(gd) @ P['rw_gup']
    rh = heads(r).astype(F32)
    vh = heads(v).astype(F32)
    kk = l2_normalize(heads(k * P['rw_kk']))
    wd_t = jnp.tanh(wd.astype(F32))
    ys = []
    finals = []
    bonus = jnp.zeros_like(vh)
    for d in range(2):
        w_log = -jax.nn.softplus(-(P['rw_w0'][d] + wd_t @ P['rw_wup'][d])) - 0.5
        decay = jnp.exp(-jnp.exp(w_log))
        a = jax.nn.sigmoid(P['rw_a0'][d] + ad.astype(F32) @ P['rw_aup'][d])
        kd = heads(k.astype(F32) * (1.0 + (a - 1.0) * P['rw_ka']))
        y, s_fin = rwkv7_scan(rh, heads(decay), kd, vh, -kk, kk * heads(a), s0[d], d == 1)
        ys.append(y)
        finals.append(s_fin)
        bonus = bonus + jnp.sum(rh * kd * P['rw_rk'], axis=-1, keepdims=True) * vh
    y = head_group_norm(ys[0] + ys[1], P['rw_lnx_g'], P['rw_lnx_b']) + bonus.reshape(bsz, n, RW_W)
    return (y * gate).astype(z.dtype), jnp.stack(finals)


def gated_delta_chunked(q, k, v, beta, g, s0):
    bsz, n, h, d = q.shape
    nc = n // GD_CHUNK

    def to_chunks(t):
        return jnp.swapaxes(t.reshape((bsz, nc, GD_CHUNK) + t.shape[2:]), 2, 3)

    q, k, v, beta, g = (to_chunks(t) for t in (q, k, v, beta, g))
    G = jnp.cumsum(g, axis=-1)
    idx = jnp.arange(GD_CHUNK)
    incl = idx[:, None] >= idx[None, :]
    strict = idx[:, None] > idx[None, :]
    decay = jnp.exp(jnp.where(incl, G[..., :, None] - G[..., None, :], -jnp.inf))
    kk = jnp.einsum('bnhik,bnhjk->bnhij', k, k)
    m = jnp.where(strict, beta[..., :, None] * kk * decay, 0.0)
    eye = jnp.eye(GD_CHUNK, dtype=F32)
    rhs = jnp.concatenate([v * beta[..., None], k * (beta * jnp.exp(G))[..., None]], axis=-1)
    sol = lax.linalg.triangular_solve(eye + m, rhs, left_side=True, lower=True, unit_diagonal=True)
    w_v, w_k = jnp.split(sol, 2, axis=-1)
    a_qk = jnp.einsum('bnhik,bnhjk->bnhij', q, k) * decay
    q_g = q * jnp.exp(G)[..., None]
    k_d = k * jnp.exp(G[..., -1:] - G)[..., None]
    g_c = jnp.exp(G[..., -1])

    def step(S, inp):
        w_v_c, w_k_c, a_c, q_c, k_c, gc = inp
        U = w_v_c - jnp.einsum('bhck,bhkv->bhcv', w_k_c, S)
        O = jnp.einsum('bhck,bhkv->bhcv', q_c, S) + jnp.einsum('bhij,bhjv->bhiv', a_c, U)
        S = gc[..., None, None] * S + jnp.einsum('bhck,bhcv->bhkv', k_c, U)
        return S, O

    xs = tuple(jnp.moveaxis(t, 1, 0) for t in (w_v, w_k, a_qk, q_g, k_d, g_c))
    s_final, O = lax.scan(step, s0, xs)
    O = jnp.swapaxes(jnp.moveaxis(O, 0, 1), 2, 3).reshape(bsz, n, h, d)
    return O, s_final


def gdn_mixer(z, P, s0):
    bsz, n, _ = z.shape
    qkv, beta_in, a_in, gate = jnp.split(z, [3 * GD_W, 3 * GD_W + 2 * GD_HEADS, 3 * GD_W + 4 * GD_HEADS], axis=-1)
    qkv = jax.nn.silu(depthwise_conv_centred(qkv, P['gd_conv']))
    q, k, v = jnp.split(qkv, 3, axis=-1)
    heads = lambda t: t.reshape(bsz, n, GD_HEADS, HEAD_DIM)
    q = l2_normalize(heads(q)) * (HEAD_DIM ** -0.5)
    k = l2_normalize(heads(k))
    v = heads(v).astype(F32)
    beta = jax.nn.sigmoid(beta_in.astype(F32)).reshape(bsz, n, 2, GD_HEADS)
    g = -jnp.exp(P['gd_a_log'].astype(F32)) * jax.nn.softplus(
        a_in.astype(F32).reshape(bsz, n, 2, GD_HEADS) + P['gd_dt_bias'])
    flip = lambda t: jnp.flip(t, 1)
    o_f, s_f = gated_delta_chunked(q, k, v, beta[:, :, 0], g[:, :, 0], s0[0])
    o_b, s_b = gated_delta_chunked(flip(q), flip(k), flip(v), flip(beta[:, :, 1]), flip(g[:, :, 1]), s0[1])
    o = o_f + flip(o_b)
    o = o * lax.rsqrt(jnp.mean(o * o, axis=-1, keepdims=True) + NORM_EPS) * P['gd_norm_g']
    o = o * jax.nn.silu(heads(gate).astype(F32))
    return o.reshape(bsz, n, GD_W).astype(z.dtype), jnp.stack([s_f, s_b])


def complex_affine_combine(e1, e2):
    a1r, a1i, b1r, b1i = e1
    a2r, a2i, b2r, b2i = e2
    return (a1r * a2r - a1i * a2i, a1r * a2i + a1i * a2r,
            a2r * b1r - a2i * b1i + b2r, a2r * b1i + a2i * b1r + b2i)


def s5_scan(u, lam_re, lam_im, log_dt, b_re, b_im, c_re, c_im, h0):
    lam_re, lam_im, b_re, b_im, c_re, c_im = (t.astype(F32) for t in (lam_re, lam_im, b_re, b_im, c_re, c_im))
    dt = jnp.exp(log_dt.astype(F32))[:, None]
    mag = jnp.exp(lam_re * dt)
    ab_re = mag * jnp.cos(lam_im * dt)
    ab_im = mag * jnp.sin(lam_im * dt)
    den = lam_re * lam_re + lam_im * lam_im
    f_re = ((ab_re - 1.0) * lam_re + ab_im * lam_im) / den
    f_im = (ab_im * lam_re - (ab_re - 1.0) * lam_im) / den
    bb_re = f_re[..., None] * b_re - f_im[..., None] * b_im
    bb_im = f_re[..., None] * b_im + f_im[..., None] * b_re
    bu_re = jnp.einsum('btgh,gph->btgp', u, bb_re)
    bu_im = jnp.einsum('btgh,gph->btgp', u, bb_im)
    h0_re, h0_im = h0[0], h0[1]
    bu_re = bu_re.at[:, 0].add(ab_re * h0_re - ab_im * h0_im)
    bu_im = bu_im.at[:, 0].add(ab_re * h0_im + ab_im * h0_re)
    n = u.shape[1]
    a_re = jnp.broadcast_to(ab_re, (1, n) + ab_re.shape)
    a_im = jnp.broadcast_to(ab_im, (1, n) + ab_im.shape)
    _, _, h_re, h_im = lax.associative_scan(complex_affine_combine, (a_re, a_im, bu_re, bu_im), axis=1)
    y = jnp.einsum('btgp,ghp->btgh', h_re, c_re) - jnp.einsum('btgp,ghp->btgh', h_im, c_im)
    return y, jnp.stack([h_re[:, -1], h_im[:, -1]])


def s5_mixer(u, P, s0, column_major):
    bsz, n, _ = u.shape
    uf = u.astype(F32)
    us = raster_to_column_major(uf) if column_major else uf
    ug = us.reshape(bsz, n, S5_GROUPS, S5_CH)
    ys = []
    finals = []
    for d in range(2):
        seq = ug if d == 0 else jnp.flip(ug, 1)
        y, h_fin = s5_scan(seq, P['s5_lam_re'][d], P['s5_lam_im'][d], P['s5_log_dt'][d],
                           P['s5_b_re'][d], P['s5_b_im'][d], P['s5_c_re'][d], P['s5_c_im'][d], s0[d])
        ys.append(y if d == 0 else jnp.flip(y, 1))
        finals.append(h_fin)
    y = (ys[0] + ys[1]).reshape(bsz, n, S5_W)
    if column_major:
        y = column_major_to_raster(y)
    y = jax.nn.gelu(y + P['s5_d'] * uf)
    y = y * jax.nn.sigmoid(y @ P['s5_glu_w'] + P['s5_glu_b'])
    return y.astype(u.dtype), jnp.stack(finals)


def token_mixers(h, P, states0, column_major):
    z = h @ P['w_in']
    z_rw, z_gd, z_s5 = jnp.split(z, [IN_RW, IN_RW + IN_GD], axis=-1)
    y_rw, s_rw = rwkv7_mixer(z_rw, P, states0[0])
    y_gd, s_gd = gdn_mixer(z_gd, P, states0[1])
    y_s5, s_s5 = s5_mixer(z_s5, P, states0[2], column_major)
    return jnp.concatenate([y_rw, y_gd, y_s5], axis=-1), (s_rw, s_gd, s_s5)


def ada_modulation(cond, w, b):
    return jnp.split(jax.nn.silu(cond) @ w + b, 6, axis=-1)


def swiglu(h, w_gate, w_up, w_down):
    return (jax.nn.silu(h @ w_gate) * (h @ w_up)) @ w_down


def setup_inputs(seed: int = 0) -> dict:
    key = jax.random.key(seed)
    ks = iter(jax.random.split(key, 64))

    def nrm(shape, scale):
        return jax.random.normal(next(ks), shape, F32) * scale

    def unif(shape, lo, hi):
        return jax.random.uniform(next(ks), shape, F32, lo, hi)

    L, D, F = DEPTH, D_MODEL, FFN_HIDDEN
    G, P, H = S5_GROUPS, S5_STATE, S5_CH
    gd_dt = jnp.exp(unif((L, 2, GD_HEADS), math.log(1e-3), math.log(1e-1)))
    return {
        'x': nrm((BATCH, SEQ, D), 1.0),
        'c': nrm((BATCH, D), 1.0),
        'ctx': nrm((BATCH, CTX_LEN, D), 1.0),
        'c_ctx': nrm((D,), 1.0),
        'norm1_g': 1.0 + nrm((L, D), 0.02),
        'norm2_g': 1.0 + nrm((L, D), 0.02),
        'final_g': 1.0 + nrm((D,), 0.02),
        'ada_w': nrm((L, D, 6 * D), D ** -0.5),
        'ada_b': nrm((L, 6 * D), 0.01),
        'w_in': nrm((L, D, IN_TOTAL), D ** -0.5),
        'w_out': nrm((L, D, D), D ** -0.5),
        'rw_mu': unif((L, IN_RW), 0.0, 1.0),
        'rw_w0': nrm((L, 2, RW_W), 0.5),
        'rw_wup': nrm((L, 2, RW_DECAY_LORA, RW_W), 0.1),
        'rw_a0': nrm((L, 2, RW_W), 0.5),
        'rw_aup': nrm((L, 2, RW_ICLR_LORA, RW_W), RW_ICLR_LORA ** -0.5),
        'rw_gup': nrm((L, RW_GATE_LORA, RW_W), RW_GATE_LORA ** -0.5),
        'rw_kk': 0.85 + nrm((L, RW_W), 0.05),
        'rw_ka': 1.0 + nrm((L, RW_W), 0.05),
        'rw_rk': nrm((L, RW_HEADS, HEAD_DIM), 0.1),
        'rw_lnx_g': 1.0 + nrm((L, RW_W), 0.02),
        'rw_lnx_b': nrm((L, RW_W), 0.01),
        'gd_conv': nrm((L, GD_CONV, 3 * GD_W), GD_CONV ** -0.5),
        'gd_a_log': jnp.log(unif((L, 2, GD_HEADS), 1.0, 16.0)),
        'gd_dt_bias': gd_dt + jnp.log(-jnp.expm1(-gd_dt)),
        'gd_norm_g': 1.0 + nrm((L, HEAD_DIM), 0.02),
        's5_lam_re': -0.5 + nrm((L, 2, G, P), 0.01),
        's5_lam_im': jnp.pi * jnp.arange(P, dtype=F32) + nrm((L, 2, G, P), 0.01),
        's5_log_dt': unif((L, 2, G), math.log(S5_DT_MIN), math.log(S5_DT_MAX)),
        's5_b_re': nrm((L, 2, G, P, H), (2 * H) ** -0.5),
        's5_b_im': nrm((L, 2, G, P, H), (2 * H) ** -0.5),
        's5_c_re': nrm((L, 2, G, H, P), P ** -0.5),
        's5_c_im': nrm((L, 2, G, H, P), P ** -0.5),
        's5_d': nrm((L, S5_W), 1.0),
        's5_glu_w': nrm((L, S5_W, S5_W), S5_W ** -0.5),
        's5_glu_b': nrm((L, S5_W), 0.01),
        'ffn_w_gate': nrm((L, D, F), D ** -0.5),
        'ffn_w_up': nrm((L, D, F), D ** -0.5),
        'ffn_w_down': nrm((L, F, D), F ** -0.5),
    }


def reference(x, c, ctx, c_ctx, norm1_g, norm2_g, final_g, ada_w, ada_b, w_in, w_out,
              rw_mu, rw_w0, rw_wup, rw_a0, rw_aup, rw_gup, rw_kk, rw_ka, rw_rk, rw_lnx_g, rw_lnx_b,
              gd_conv, gd_a_log, gd_dt_bias, gd_norm_g,
              s5_lam_re, s5_lam_im, s5_log_dt, s5_b_re, s5_b_im, s5_c_re, s5_c_im, s5_d, s5_glu_w, s5_glu_b,
              ffn_w_gate, ffn_w_up, ffn_w_down):
    bsz = x.shape[0]
    zero_states = (jnp.zeros((2, bsz, RW_HEADS, HEAD_DIM, HEAD_DIM), F32),
                   jnp.zeros((2, bsz, GD_HEADS, HEAD_DIM, HEAD_DIM), F32),
                   jnp.zeros((2, 2, bsz, S5_GROUPS, S5_STATE), F32))
    for l in range(DEPTH):
        P = {
            'w_in': w_in[l], 'rw_mu': rw_mu[l], 'rw_w0': rw_w0[l], 'rw_wup': rw_wup[l], 'rw_a0': rw_a0[l],
            'rw_aup': rw_aup[l], 'rw_gup': rw_gup[l], 'rw_kk': rw_kk[l], 'rw_ka': rw_ka[l], 'rw_rk': rw_rk[l],
            'rw_lnx_g': rw_lnx_g[l], 'rw_lnx_b': rw_lnx_b[l], 'gd_conv': gd_conv[l], 'gd_a_log': gd_a_log[l],
            'gd_dt_bias': gd_dt_bias[l], 'gd_norm_g': gd_norm_g[l], 's5_lam_re': s5_lam_re[l],
            's5_lam_im': s5_lam_im[l], 's5_log_dt': s5_log_dt[l], 's5_b_re': s5_b_re[l], 's5_b_im': s5_b_im[l],
            's5_c_re': s5_c_re[l], 's5_c_im': s5_c_im[l], 's5_d': s5_d[l], 's5_glu_w': s5_glu_w[l],
            's5_glu_b': s5_glu_b[l],
        }
        sh1c, sc1c, g1c, sh2c, sc2c, g2c = ada_modulation(c_ctx[None, None, :], ada_w[l], ada_b[l])
        hc = rms_norm(ctx, norm1_g[l]) * (1.0 + sc1c) + sh1c
        mix_c, ctx_states = token_mixers(hc, P, zero_states, False)
        sh1x, sc1x, g1x, sh2x, sc2x, g2x = ada_modulation(c[:, None, :], ada_w[l], ada_b[l])
        hx = rms_norm(x, norm1_g[l]) * (1.0 + sc1x) + sh1x
        mix_x, _ = token_mixers(hx, P, ctx_states, True)
        x = x + g1x * (mix_x @ w_out[l])
        hx2 = rms_norm(x, norm2_g[l]) * (1.0 + sc2x) + sh2x
        x = x + g2x * swiglu(hx2, ffn_w_gate[l], ffn_w_up[l], ffn_w_down[l])
        if l < DEPTH - 1:
            ctx = ctx + g1c * (mix_c @ w_out[l])
            hc2 = rms_norm(ctx, norm2_g[l]) * (1.0 + sc2c) + sh2c
            ctx = ctx + g2c * swiglu(hc2, ffn_w_gate[l], ffn_w_up[l], ffn_w_down[l])
    return rms_norm(x, final_g)
```

```python
import functools

import jax
import jax.numpy as jnp
from jax import lax
from jax.experimental import pallas as pl
from jax.experimental.pallas import tpu as pltpu

F32 = jnp.float32
MXU_DTYPE = jnp.bfloat16
HIGHEST = lax.Precision.HIGHEST

NORM_EPS = 1e-6
LNX_EPS = 64e-5
HEAD_DIM = 64
GRID_W = 64
CHUNK = 64
SUBLANES = 8
LANES = 128
RW_DECAY_LORA = 64
RW_ICLR_LORA = 64
RW_GATE_LORA = 128
S5_CH = 16
S5_STATE = 64
VMEM_LIMIT = 56 * 1024 * 1024


def _cparams(*sem):
    return pltpu.CompilerParams(dimension_semantics=sem, vmem_limit_bytes=VMEM_LIMIT)


def _sigmoid(x):
    return 1.0 / (1.0 + jnp.exp(-x))


def _silu(x):
    return x * _sigmoid(x)


def _softplus(x):
    return jnp.maximum(x, 0.0) + jnp.log(1.0 + jnp.exp(-jnp.abs(x)))


def _gelu_tanh(x):
    return 0.5 * x * (1.0 + jnp.tanh(0.7978845608028654 * (x + 0.044715 * x * x * x)))


def _mm(a, b):
    return jnp.dot(a.astype(MXU_DTYPE), b.astype(MXU_DTYPE), preferred_element_type=F32)


def _mm_nt(a, b):
    return lax.dot_general(a.astype(MXU_DTYPE), b.astype(MXU_DTYPE), (((1,), (1,)), ((), ())),
                           preferred_element_type=F32)


def _mm_tn(a, b):
    return lax.dot_general(a.astype(MXU_DTYPE), b.astype(MXU_DTYPE), (((0,), (0,)), ((), ())),
                           preferred_element_type=F32)


def _mm_hi(a, b):
    return jnp.dot(a, b, precision=HIGHEST, preferred_element_type=F32)


def _mm_nt_hi(a, b):
    return lax.dot_general(a, b, (((1,), (1,)), ((), ())), precision=HIGHEST,
                           preferred_element_type=F32)


def _head_mean_matrix(width):
    r = lax.broadcasted_iota(jnp.int32, (width, width), 0) // HEAD_DIM
    c = lax.broadcasted_iota(jnp.int32, (width, width), 1) // HEAD_DIM
    return jnp.where(r == c, 1.0 / HEAD_DIM, 0.0).astype(F32)


def _tri_masks(n, reverse):
    t = lax.broadcasted_iota(jnp.int32, (n, n), 0)
    j = lax.broadcasted_iota(jnp.int32, (n, n), 1)
    if reverse:
        return j >= t, j > t, t == j
    return j <= t, j < t, t == j


def _shifted(z, prev_row, next_row):
    n = z.shape[0]
    row = lax.broadcasted_iota(jnp.int32, z.shape, 0)
    zp = jnp.where(row == 0, prev_row, pltpu.roll(z, 1, axis=0))
    zn = jnp.where(row == n - 1, next_row, pltpu.roll(z, n - 1, axis=0))
    return zp, zn


def _unit_triangular_inverse(nmat, eye):
    c = nmat.shape[0]
    t = lax.broadcasted_iota(jnp.int32, (c, c), 0)
    j = lax.broadcasted_iota(jnp.int32, (c, c), 1)
    s = SUBLANES
    dmat = jnp.where(t // s == j // s, nmat, 0.0)
    x = eye + dmat
    p = dmat
    for _ in range(2):
        p = _mm(p, p)
        x = x + _mm(p, x)
    while s < c:
        cm = jnp.where((t // (2 * s) == j // (2 * s)) & (t // s != j // s), nmat, 0.0)
        x = x + _mm(x, _mm(cm, x))
        s *= 2
    return x


def _chunk_tail(at, rt, a_ab, a_ak, a_rb, a_rk, v, bg, kg, decay_row, s, eye):
    n = v.shape[1]
    x = _unit_triangular_inverse(a_ab, eye)
    akv = _mm(a_ak, v)
    w = _mm(x, jnp.concatenate([at, akv], axis=1))
    u = _mm_nt(w[:, :n], s) + w[:, n:]
    y = _mm_nt(rt, s) + _mm(a_rb, u) + _mm(a_rk, v)
    s_new = s * decay_row + _mm_tn(jnp.concatenate([u, v], axis=0), jnp.concatenate([bg, kg], axis=0))
    return y, s_new


def _dplr_chunk(r, lw, k, v, a, b, s, masks):
    incl, strict, diag = masks
    c = r.shape[0]
    g = _mm_hi(incl.astype(F32), lw)
    gt = jnp.sum(lw, axis=0, keepdims=True)
    e_neg = jnp.exp(-g)
    e_rem = jnp.exp(gt - g)
    rt = r * jnp.exp(g)
    at = a * jnp.exp(g - lw)
    bt = b * e_neg
    kt = k * e_neg
    am = _mm_nt(jnp.concatenate([at, rt], axis=0), jnp.concatenate([bt, kt], axis=0))
    a_ab = jnp.where(strict, am[:c, :c], 0.0)
    a_ak = jnp.where(strict, am[:c, c:], 0.0)
    a_rb = jnp.where(incl, am[c:, :c], 0.0)
    a_rk = jnp.where(incl, am[c:, c:], 0.0)
    return _chunk_tail(at, rt, a_ab, a_ak, a_rb, a_rk, v, b * e_rem, k * e_rem, jnp.exp(gt), s,
                       diag.astype(F32))


def _gdn_chunk(q, k, v, bk, g_col, lg_col, g_row, lg_row, s, masks):
    incl, strict, diag = masks
    c = q.shape[0]
    gx_col = g_col - lg_col
    gx_row = g_row - lg_row

    def dec(col, row, mask):
        return jnp.where(mask, jnp.exp(jnp.minimum(col - row, 0.0)), 0.0)

    gram = _mm_nt(jnp.concatenate([k, q], axis=0), bk)
    kbk = gram[:c]
    qbk = gram[c:]
    a_ab = -kbk * dec(gx_col, gx_row, strict)
    a_ak = kbk * dec(gx_col, g_row, strict)
    a_rb = -qbk * dec(g_col, gx_row, incl)
    a_rk = qbk * dec(g_col, g_row, incl)
    gt = jnp.sum(lg_col, axis=0, keepdims=True)
    at = k * jnp.exp(gx_col)
    rt = q * jnp.exp(g_col)
    bg = -bk * jnp.exp(gt - gx_col)
    kg = bk * jnp.exp(gt - g_col)
    return _chunk_tail(at, rt, a_ab, a_ak, a_rb, a_rk, v, bg, kg, jnp.exp(gt), s, diag.astype(F32))


def _ada_kernel(cond_ref, w_ref, b_ref, o_ref):
    cnd = cond_ref[...]
    o_ref[...] = _mm_hi(_silu(cnd), w_ref[...]) + b_ref[...]


def _ada_modulation(cond, ada_w, ada_b):
    depth, d, n6 = ada_w.shape
    tn = n6 // 4
    return pl.pallas_call(
        _ada_kernel,
        grid=(depth, n6 // tn),
        in_specs=[pl.BlockSpec((SUBLANES, d), lambda l, j: (0, 0)),
                  pl.BlockSpec((None, d, tn), lambda l, j: (l, 0, j)),
                  pl.BlockSpec((None, 1, tn), lambda l, j: (l, 0, j))],
        out_specs=pl.BlockSpec((None, SUBLANES, tn), lambda l, j: (l, 0, j)),
        out_shape=jax.ShapeDtypeStruct((depth, SUBLANES, n6), F32),
        compiler_params=_cparams("parallel", "parallel"),
    )(cond, ada_w, ada_b.reshape(depth, 1, n6))


def _in_kernel(splits, x_ref, mod_ref, g_ref, w_ref, *o_refs):
    x = x_ref[...]
    ms = jnp.mean(x * x, axis=-1, keepdims=True)
    h = x * lax.rsqrt(ms + NORM_EPS) * g_ref[...]
    h = h * (1.0 + mod_ref[1:2, :]) + mod_ref[0:1, :]
    z = _mm(h, w_ref[...])
    off = 0
    for o_ref, width in zip(o_refs, splits):
        o_ref[...] = z[:, off:off + width]
        off += width


def _in_projection(xs, mod, mod_row, gain, w_perm, splits):
    bsz, n, d = xs.shape
    tm = min(512, n)
    total = sum(splits)
    return pl.pallas_call(
        functools.partial(_in_kernel, splits),
        grid=(bsz, n // tm),
        in_specs=[pl.BlockSpec((None, tm, d), lambda b, i: (b, i, 0)),
                  pl.BlockSpec((None, 6, d), lambda b, i: (mod_row(b), 0, 0)),
                  pl.BlockSpec((1, d), lambda b, i: (0, 0)),
                  pl.BlockSpec((d, total), lambda b, i: (0, 0))],
        out_specs=[pl.BlockSpec((None, tm, w), lambda b, i: (b, i, 0)) for w in splits],
        out_shape=[jax.ShapeDtypeStruct((bsz, n, w), F32) for w in splits],
        compiler_params=_cparams("parallel", "parallel"),
    )(xs, mod, gain, w_perm)


def _chunk_specs(width, nc, cpb):
    last_tile = nc * cpb - 1
    specs = []
    for rev in (False, True):
        ci = (lambda i: nc - 1 - i) if rev else (lambda i: i)
        specs.append(pl.BlockSpec((None, CHUNK, width), lambda b, i, ci=ci: (b, ci(i), 0)))
        specs.append(pl.BlockSpec((None, SUBLANES, width),
                                  lambda b, i, ci=ci: (b, jnp.maximum(ci(i) * cpb - 1, 0), 0)))
        specs.append(pl.BlockSpec((None, SUBLANES, width),
                                  lambda b, i, ci=ci: (b, jnp.minimum((ci(i) + 1) * cpb, last_tile), 0)))
    return specs


def _halo_rows(p_ref, n_ref, ci, nc):
    prev_row = jnp.where(ci > 0, p_ref[SUBLANES - 1:SUBLANES, :], 0.0)
    next_row = jnp.where(ci < nc - 1, n_ref[0:1, :], 0.0)
    return prev_row, next_row


def _rw_kernel(nc, heads,
               zf_ref, zfp_ref, zfn_ref, zb_ref, zbp_ref, zbn_ref,
               mu_ref, w0_ref, wup_ref, a0_ref, aup_ref, gup_ref, kk_ref, ka_ref, rk_ref, s0_ref,
               yf_ref, yb_ref, bonf_ref, bonb_ref, gate_ref, sfin_ref, s_scr):
    i = pl.program_id(1)
    w = heads * HEAD_DIM

    @pl.when(i == 0)
    def _():
        s_scr[...] = s0_ref[...]

    hmean = _head_mean_matrix(w)
    views = ((zf_ref, zfp_ref, zfn_ref, i, yf_ref, bonf_ref),
             (zb_ref, zbp_ref, zbn_ref, nc - 1 - i, yb_ref, bonb_ref))
    for d, (z_ref, p_ref, n_ref, ci, y_ref, bon_ref) in enumerate(views):
        z = z_ref[...]
        zp, zn = _shifted(z, *_halo_rows(p_ref, n_ref, ci, nc))
        z = z + (0.5 * (zp + zn) - z) * mu_ref[...]
        r = z[:, :w]
        k = z[:, w:2 * w]
        v = z[:, 2 * w:3 * w]
        o = 3 * w
        wd = z[:, o:o + RW_DECAY_LORA]
        ad = z[:, o + RW_DECAY_LORA:o + RW_DECAY_LORA + RW_ICLR_LORA]
        gd = z[:, o + RW_DECAY_LORA + RW_ICLR_LORA:]
        if d == 0:
            gate_ref[...] = _mm(_sigmoid(gd), gup_ref[...])
        kk = k * kk_ref[...]
        kk = kk * lax.rsqrt(_mm_hi(kk * kk, hmean) * HEAD_DIM + NORM_EPS)
        w_log = -_softplus(-(w0_ref[d:d + 1, :] + _mm(jnp.tanh(wd), wup_ref[d]))) - 0.5
        lw = -jnp.exp(w_log)
        a = _sigmoid(a0_ref[d:d + 1, :] + _mm(ad, aup_ref[d]))
        kd = k * (1.0 + (a - 1.0) * ka_ref[...])
        bon_ref[...] = _mm_hi(r * kd * rk_ref[...], hmean) * HEAD_DIM * v
        b = kk * a
        masks = _tri_masks(CHUNK, d == 1)
        ys = []
        for h in range(heads):
            sl = slice(h * HEAD_DIM, (h + 1) * HEAD_DIM)
            y, s_new = _dplr_chunk(r[:, sl], lw[:, sl], kd[:, sl], v[:, sl], -kk[:, sl], b[:, sl],
                                   s_scr[d, h], masks)
            s_scr[d, h] = s_new
            ys.append(y)
        y_ref[...] = jnp.concatenate(ys, axis=1)

    @pl.when(i == nc - 1)
    def _():
        sfin_ref[...] = s_scr[...]


def _rwkv_scan(z_rw, p, s0):
    bsz, n, zw = z_rw.shape
    nc = n // CHUNK
    w = p['rw_kk'].shape[-1]
    heads = w // HEAD_DIM
    full = lambda a: pl.BlockSpec(a.shape, lambda b, i, nd=a.ndim: (0,) * nd)
    params = [p['rw_mu'], p['rw_w0'], p['rw_wup'], p['rw_a0'], p['rw_aup'], p['rw_gup'],
              p['rw_kk'], p['rw_ka'], p['rw_rk']]
    state_spec = pl.BlockSpec((2, None, heads, HEAD_DIM, HEAD_DIM), lambda b, i: (0, b, 0, 0, 0))
    fwd = pl.BlockSpec((None, CHUNK, w), lambda b, i: (b, i, 0))
    bwd = pl.BlockSpec((None, CHUNK, w), lambda b, i: (b, nc - 1 - i, 0))
    tok = jax.ShapeDtypeStruct((bsz, n, w), F32)
    return pl.pallas_call(
        functools.partial(_rw_kernel, nc, heads),
        grid=(bsz, nc),
        in_specs=_chunk_specs(zw, nc, CHUNK // SUBLANES) + [full(a) for a in params] + [state_spec],
        out_specs=[fwd, bwd, fwd, bwd, fwd, state_spec],
        out_shape=[tok, tok, tok, tok, tok, jax.ShapeDtypeStruct(s0.shape, F32)],
        scratch_shapes=[pltpu.VMEM((2, heads, HEAD_DIM, HEAD_DIM), F32)],
        compiler_params=_cparams("parallel", "arbitrary"),
    )(z_rw, z_rw, z_rw, z_rw, z_rw, z_rw, *params, s0)


def _gd_kernel(nc, heads,
               zf_ref, zfp_ref, zfn_ref, zb_ref, zbp_ref, zbn_ref, baf_ref, bab_ref,
               conv_ref, alog_ref, dtb_ref, s0_ref,
               of_ref, ob_ref, sfin_ref, s_scr):
    i = pl.program_id(1)
    w = heads * HEAD_DIM

    @pl.when(i == 0)
    def _():
        s_scr[...] = s0_ref[...]

    hmean = _head_mean_matrix(w)
    lane_eye = _tri_masks(LANES, False)[2].astype(F32)
    views = ((zf_ref, zfp_ref, zfn_ref, baf_ref, i, of_ref),
             (zb_ref, zbp_ref, zbn_ref, bab_ref, nc - 1 - i, ob_ref))
    for d, (z_ref, p_ref, n_ref, ba_ref, ci, o_ref) in enumerate(views):
        z = z_ref[...]
        zp, zn = _shifted(z, *_halo_rows(p_ref, n_ref, ci, nc))
        x = _silu(conv_ref[0:1, :] * zp + conv_ref[1:2, :] * z + conv_ref[2:3, :] * zn)
        q = x[:, :w]
        k = x[:, w:2 * w]
        v = x[:, 2 * w:]
        q = q * lax.rsqrt(_mm_hi(q * q, hmean) * HEAD_DIM + NORM_EPS) * (HEAD_DIM ** -0.5)
        k = k * lax.rsqrt(_mm_hi(k * k, hmean) * HEAD_DIM + NORM_EPS)
        ba = ba_ref[...]
        beta_all = _sigmoid(ba)
        lg_all = -jnp.exp(alog_ref[...]) * _softplus(ba + dtb_ref[...])
        masks = _tri_masks(CHUNK, d == 1)
        g_all = _mm_hi(masks[0].astype(F32), lg_all)
        rows = _mm_nt_hi(lane_eye, jnp.concatenate([g_all, lg_all], axis=0))
        os_ = []
        for h in range(heads):
            sl = slice(h * HEAD_DIM, (h + 1) * HEAD_DIM)
            cb = d * heads + h
            cg = 2 * heads + cb
            bk = k[:, sl] * beta_all[:, cb:cb + 1]
            o, s_new = _gdn_chunk(q[:, sl], k[:, sl], v[:, sl], bk,
                                  g_all[:, cg:cg + 1], lg_all[:, cg:cg + 1],
                                  rows[cg:cg + 1, :CHUNK], rows[cg:cg + 1, CHUNK:],
                                  s_scr[d, h], masks)
            s_scr[d, h] = s_new
            os_.append(o)
        o_ref[...] = jnp.concatenate(os_, axis=1)

    @pl.when(i == nc - 1)
    def _():
        sfin_ref[...] = s_scr[...]


def _gdn_scan(z_q, z_ba, p, s0):
    bsz, n, zw = z_q.shape
    nc = n // CHUNK
    w = zw // 3
    heads = w // HEAD_DIM
    full = lambda a: pl.BlockSpec(a.shape, lambda b, i, nd=a.ndim: (0,) * nd)
    pad = lambda t: jnp.pad(t.reshape(1, 2 * heads), ((0, 0), (2 * heads, LANES - 4 * heads)))
    params = [p['gd_conv'], pad(p['gd_a_log']), pad(p['gd_dt_bias'])]
    state_spec = pl.BlockSpec((2, None, heads, HEAD_DIM, HEAD_DIM), lambda b, i: (0, b, 0, 0, 0))
    fwd = pl.BlockSpec((None, CHUNK, w), lambda b, i: (b, i, 0))
    bwd = pl.BlockSpec((None, CHUNK, w), lambda b, i: (b, nc - 1 - i, 0))
    ba_f = pl.BlockSpec((None, CHUNK, LANES), lambda b, i: (b, i, 0))
    ba_b = pl.BlockSpec((None, CHUNK, LANES), lambda b, i: (b, nc - 1 - i, 0))
    tok = jax.ShapeDtypeStruct((bsz, n, w), F32)
    return pl.pallas_call(
        functools.partial(_gd_kernel, nc, heads),
        grid=(bsz, nc),
        in_specs=_chunk_specs(zw, nc, CHUNK // SUBLANES) + [ba_f, ba_b] + [full(a) for a in params]
        + [state_spec],
        out_specs=[fwd, bwd, state_spec],
        out_shape=[tok, tok, jax.ShapeDtypeStruct(s0.shape, F32)],
        scratch_shapes=[pltpu.VMEM((2, heads, HEAD_DIM, HEAD_DIM), F32)],
        compiler_params=_cparams("parallel", "arbitrary"),
    )(z_q, z_q, z_q, z_q, z_q, z_q, z_ba, z_ba, *params, s0)


def _s5_disc_kernel(lre_ref, lim_ref, ldt_ref, bre_ref, bim_ref, abr_ref, abi_ref, bbr_ref, bbi_ref):
    lre = lre_ref[...]
    lim = lim_ref[...]
    dt = jnp.exp(ldt_ref[...])
    mag = jnp.exp(lre * dt)
    abr = mag * jnp.cos(lim * dt)
    abi = mag * jnp.sin(lim * dt)
    den = lre * lre + lim * lim
    fr = ((abr - 1.0) * lre + abi * lim) / den
    fi = (abi * lre - (abr - 1.0) * lim) / den
    abr_ref[...] = abr
    abi_ref[...] = abi
    bbr_ref[...] = fr * bre_ref[...] - fi * bim_ref[...]
    bbi_ref[...] = fr * bim_ref[...] + fi * bre_ref[...]


def _s5_discretise(p):
    _, groups, state = p['s5_lam_re'].shape
    ch = p['s5_b_re'].shape[-1]
    rows = 2 * groups * ch
    spread = lambda t: jnp.broadcast_to(t[:, :, None, :], (2, groups, ch, state)).reshape(rows, state)
    lre = spread(p['s5_lam_re'])
    lim = spread(p['s5_lam_im'])
    ldt = spread(jnp.broadcast_to(p['s5_log_dt'][:, :, None], (2, groups, state)))
    bre = jnp.swapaxes(p['s5_b_re'], 2, 3).reshape(rows, state)
    bim = jnp.swapaxes(p['s5_b_im'], 2, 3).reshape(rows, state)
    out = jax.ShapeDtypeStruct((rows, state), F32)
    abr, abi, bbr, bbi = pl.pallas_call(_s5_disc_kernel, out_shape=[out] * 4)(lre, lim, ldt, bre, bim)
    eye = jnp.eye(groups, dtype=F32)
    chain = lambda t: jnp.repeat(t.reshape(2, groups, ch, state)[:, :, 0, :].reshape(2, groups * state),
                                 SUBLANES // 2, axis=0)

    def bmat(t):
        t = t.reshape(2, groups, ch, state)
        return (t[:, :, :, None, :] * eye[None, :, None, :, None]).reshape(2, groups * ch, groups * state)

    def cmat(t):
        t = jnp.swapaxes(t, 2, 3)
        return (t[:, :, :, None, :] * eye[None, :, None, :, None]).reshape(2, groups * state, groups * ch)

    b_all = jnp.concatenate([bmat(bbr), bmat(bbi)], axis=-1).astype(MXU_DTYPE)
    c_all = jnp.stack([cmat(p['s5_c_re']), cmat(p['s5_c_im'])], axis=1).astype(MXU_DTYPE)
    return chain(abr), chain(abi), b_all, c_all


def _s5_kernel(tc, u_ref, b_ref, c_ref, ar_ref, ai_ref, h0r_ref, h0i_ref,
               y_ref, hfr_ref, hfi_ref, bur, bui, hr, hi):
    i = pl.program_id(0)
    rows = tc * SUBLANES
    ns = ar_ref.shape[1]

    @pl.when(i == 0)
    def _():
        hr[...] = h0r_ref[...]
        hi[...] = h0i_ref[...]

    u = u_ref[...].reshape(rows, u_ref.shape[-1])
    fwd_row = (lax.broadcasted_iota(jnp.int32, (rows, 1), 0) % SUBLANES) < (SUBLANES // 2)
    bu = jnp.where(fwd_row, _mm(u, b_ref[0]), _mm(u, b_ref[1]))
    bur[...] = bu[:, :ns]
    bui[...] = bu[:, ns:]
    cw = 512
    for cb in range(ns // cw):
        cols = slice(cb * cw, (cb + 1) * cw)
        ar = ar_ref[:, cols]
        ai = ai_ref[:, cols]

        def body(t, carry, cols=cols, ar=ar, ai=ai):
            h_r, h_i = carry
            o = pl.multiple_of(t * SUBLANES, SUBLANES)
            n_r = ar * h_r - ai * h_i + bur[pl.ds(o, SUBLANES), cols]
            n_i = ar * h_i + ai * h_r + bui[pl.ds(o, SUBLANES), cols]
            bur[pl.ds(o, SUBLANES), cols] = n_r
            bui[pl.ds(o, SUBLANES), cols] = n_i
            return n_r, n_i

        h_r, h_i = lax.fori_loop(0, tc, body, (hr[:, cols], hi[:, cols]), unroll=8)
        hr[:, cols] = h_r
        hi[:, cols] = h_i
    h_re = bur[...]
    h_im = bui[...]
    y = jnp.where(fwd_row,
                  _mm(h_re, c_ref[0, 0]) - _mm(h_im, c_ref[0, 1]),
                  _mm(h_re, c_ref[1, 0]) - _mm(h_im, c_ref[1, 1]))
    y_ref[...] = y.reshape(y_ref.shape)
    hfr_ref[...] = hr[...]
    hfi_ref[...] = hi[...]


def _s5_scan(u2, disc, h0):
    ar, ai, b_all, c_all = disc
    n, chains, w = u2.shape
    ns = ar.shape[1]
    tc = min(64, n)
    full = lambda a: pl.BlockSpec(a.shape, lambda i, nd=a.ndim: (0,) * nd)
    st = jax.ShapeDtypeStruct((chains, ns), F32)
    y2, hfr, hfi = pl.pallas_call(
        functools.partial(_s5_kernel, tc),
        grid=(n // tc,),
        in_specs=[pl.BlockSpec((tc, chains, w), lambda i: (i, 0, 0)),
                  full(b_all), full(c_all), full(ar), full(ai), full(h0[0]), full(h0[1])],
        out_specs=[pl.BlockSpec((tc, chains, w), lambda i: (i, 0, 0)), full(ar), full(ar)],
        out_shape=[jax.ShapeDtypeStruct((n, chains, w), F32), st, st],
        scratch_shapes=[pltpu.VMEM((tc * chains, ns), F32), pltpu.VMEM((tc * chains, ns), F32),
                        pltpu.VMEM((chains, ns), F32), pltpu.VMEM((chains, ns), F32)],
        compiler_params=_cparams("arbitrary"),
    )(u2, b_all, c_all, ar, ai, h0[0], h0[1])
    return y2, (hfr, hfi)


def _to_scan_order(u, column_major):
    bsz, n, w = u.shape
    if column_major:
        return u.reshape(bsz, n // GRID_W, GRID_W, w).transpose(2, 1, 0, 3).reshape(n, bsz, w)
    return u.transpose(1, 0, 2)


def _from_scan_order(y, column_major):
    n, bsz, w = y.shape
    if column_major:
        return y.reshape(GRID_W, n // GRID_W, bsz, w).transpose(2, 1, 0, 3).reshape(bsz, n, w)
    return y.transpose(1, 0, 2)


def _mixout_kernel(x_ref, mod_ref, yf_ref, yb_ref, bonf_ref, bonb_ref, gate_ref, of_ref, ob_ref, zg_ref,
                   s5f_ref, s5b_ref, u5_ref, lnxg_ref, lnxb_ref, gdg_ref, s5d_ref, gluw_ref, glub_ref,
                   wout_ref, o_ref):
    hmean = _head_mean_matrix(yf_ref.shape[-1])
    y = yf_ref[...] + yb_ref[...]
    yc = y - _mm_hi(y, hmean)
    yn = yc * lax.rsqrt(_mm_hi(yc * yc, hmean) + LNX_EPS)
    y_rw = (yn * lnxg_ref[...] + lnxb_ref[...] + bonf_ref[...] + bonb_ref[...]) * gate_ref[...]
    o = of_ref[...] + ob_ref[...]
    o = o * lax.rsqrt(_mm_hi(o * o, hmean) + NORM_EPS) * gdg_ref[...]
    y_gd = o * _silu(zg_ref[...])
    y5 = _gelu_tanh(s5f_ref[...] + s5b_ref[...] + s5d_ref[...] * u5_ref[...])
    y5 = y5 * _sigmoid(_mm(y5, gluw_ref[...]) + glub_ref[...])
    mix = jnp.concatenate([y_rw, y_gd, y5], axis=1)
    o_ref[...] = x_ref[...] + mod_ref[2:3, :] * _mm(mix, wout_ref[...])


def _mix_out(xs, mod, mod_row, toks, params):
    bsz, n, d = xs.shape
    tm = min(512, n)
    tok_spec = lambda a: pl.BlockSpec((None, tm, a.shape[-1]), lambda b, i: (b, i, 0))
    full = lambda a: pl.BlockSpec(a.shape, lambda b, i, nd=a.ndim: (0,) * nd)
    return pl.pallas_call(
        _mixout_kernel,
        grid=(bsz, n // tm),
        in_specs=[tok_spec(xs), pl.BlockSpec((None, 6, d), lambda b, i: (mod_row(b), 0, 0))]
        + [tok_spec(a) for a in toks] + [full(a) for a in params],
        out_specs=tok_spec(xs),
        out_shape=jax.ShapeDtypeStruct(xs.shape, F32),
        compiler_params=_cparams("parallel", "parallel"),
    )(xs, mod, *toks, *params)


def _ffn_kernel(hidden, final, x_ref, mod_ref, g_ref, wgu_ref, wd_ref, fg_ref, o_ref):
    x = x_ref[...]
    ms = jnp.mean(x * x, axis=-1, keepdims=True)
    h = x * lax.rsqrt(ms + NORM_EPS) * g_ref[...]
    h = h * (1.0 + mod_ref[4:5, :]) + mod_ref[3:4, :]
    gu = _mm(h, wgu_ref[...])
    act = _silu(gu[:, :hidden]) * gu[:, hidden:]
    o = x + mod_ref[5:6, :] * _mm(act, wd_ref[...])
    if final:
        ms = jnp.mean(o * o, axis=-1, keepdims=True)
        o = o * lax.rsqrt(ms + NORM_EPS) * fg_ref[...]
    o_ref[...] = o


def _ffn(xs, mod, mod_row, gain, w_gu, w_down, final_gain, final):
    bsz, n, d = xs.shape
    hidden = w_down.shape[0]
    tm = min(256, n)
    const = lambda a: pl.BlockSpec(a.shape, lambda b, i, nd=a.ndim: (0,) * nd,
                                   pipeline_mode=pl.Buffered(1))
    tok = pl.BlockSpec((None, tm, d), lambda b, i: (b, i, 0))
    return pl.pallas_call(
        functools.partial(_ffn_kernel, hidden, final),
        grid=(bsz, n // tm),
        in_specs=[tok, pl.BlockSpec((None, 6, d), lambda b, i: (mod_row(b), 0, 0)),
                  const(gain), const(w_gu), const(w_down), const(final_gain)],
        out_specs=tok,
        out_shape=jax.ShapeDtypeStruct(xs.shape, F32),
        compiler_params=_cparams("parallel", "parallel"),
    )(xs, mod, gain, w_gu, w_down, final_gain)


def kernel(x, c, ctx, c_ctx, norm1_g, norm2_g, final_g, ada_w, ada_b, w_in, w_out,
           rw_mu, rw_w0, rw_wup, rw_a0, rw_aup, rw_gup, rw_kk, rw_ka, rw_rk, rw_lnx_g, rw_lnx_b,
           gd_conv, gd_a_log, gd_dt_bias, gd_norm_g,
           s5_lam_re, s5_lam_im, s5_log_dt, s5_b_re, s5_b_im, s5_c_re, s5_c_im, s5_d, s5_glu_w, s5_glu_b,
           ffn_w_gate, ffn_w_up, ffn_w_down):
    bsz, n, d = x.shape
    depth = w_in.shape[0]
    rw_w = rw_kk.shape[-1]
    gd_w = gd_conv.shape[-1] // 3
    s5_w = s5_d.shape[-1]
    rw_heads = rw_w // HEAD_DIM
    gd_heads = gd_w // HEAD_DIM
    in_rw = rw_mu.shape[-1]
    assert 2 * bsz == SUBLANES and n % GRID_W == 0 and n % CHUNK == 0 and ctx.shape[1] % CHUNK == 0

    cond = jnp.concatenate([c, c_ctx[None, :], jnp.zeros((SUBLANES - bsz - 1, d), F32)], axis=0)
    mod_all = _ada_modulation(cond, ada_w, ada_b).reshape(depth, SUBLANES, 6, d)
    x_row = lambda b: b
    ctx_row = lambda b: bsz

    o_gd = in_rw
    o_ba = o_gd + 3 * gd_w
    o_gate = o_ba + 4 * gd_heads
    o_s5 = o_gate + gd_w
    splits = (in_rw, 3 * gd_w, gd_w, s5_w, LANES)
    w_perm = jnp.concatenate(
        [w_in[:, :, :o_gd], w_in[:, :, o_gd:o_ba], w_in[:, :, o_gate:o_s5], w_in[:, :, o_s5:],
         w_in[:, :, o_ba:o_gate], jnp.zeros((depth, d, LANES - 4 * gd_heads), F32)], axis=-1).astype(MXU_DTYPE)
    w_gu = jnp.concatenate([ffn_w_gate, ffn_w_up], axis=-1).astype(MXU_DTYPE)
    w_dn = ffn_w_down.astype(MXU_DTYPE)
    w_o = w_out.astype(MXU_DTYPE)
    glu_w = s5_glu_w.astype(MXU_DTYPE)
    row = lambda t: t.reshape(1, -1)

    for l in range(depth):
        p = {'rw_mu': row(rw_mu[l]), 'rw_w0': rw_w0[l], 'rw_wup': rw_wup[l], 'rw_a0': rw_a0[l],
             'rw_aup': rw_aup[l], 'rw_gup': rw_gup[l], 'rw_kk': row(rw_kk[l]), 'rw_ka': row(rw_ka[l]),
             'rw_rk': row(rw_rk[l]), 'gd_conv': gd_conv[l], 'gd_a_log': gd_a_log[l],
             'gd_dt_bias': gd_dt_bias[l], 's5_lam_re': s5_lam_re[l], 's5_lam_im': s5_lam_im[l],
             's5_log_dt': s5_log_dt[l], 's5_b_re': s5_b_re[l], 's5_b_im': s5_b_im[l],
             's5_c_re': s5_c_re[l], 's5_c_im': s5_c_im[l]}
        mix_params = [row(rw_lnx_g[l]), row(rw_lnx_b[l]), row(jnp.tile(gd_norm_g[l], gd_heads)),
                      row(s5_d[l]), glu_w[l], row(s5_glu_b[l]), w_o[l]]
        disc = _s5_discretise(p)
        mod = mod_all[l]
        states = (jnp.zeros((2, bsz, rw_heads, HEAD_DIM, HEAD_DIM), F32),
                  jnp.zeros((2, bsz, gd_heads, HEAD_DIM, HEAD_DIM), F32),
                  (jnp.zeros((SUBLANES, disc[0].shape[1]), F32),) * 2)
        new_streams = []
        for stream, mod_row, column_major, is_ctx in ((ctx, ctx_row, False, True), (x, x_row, True, False)):
            z_rw, z_q, z_g, z_s5, z_ba = _in_projection(stream, mod, mod_row, row(norm1_g[l]), w_perm[l], splits)
            yf, yb, bonf, bonb, gate, s_rw = _rwkv_scan(z_rw, p, states[0])
            of, ob, s_gd = _gdn_scan(z_q, z_ba, p, states[1])
            ut = _to_scan_order(z_s5, column_major)
            y2, s_s5 = _s5_scan(jnp.concatenate([ut, ut[::-1]], axis=1), disc, states[2])
            s5f = _from_scan_order(y2[:, :bsz], column_major)
            s5b = _from_scan_order(y2[::-1, bsz:], column_major)
            states = (s_rw, s_gd, s_s5)
            if is_ctx and l == depth - 1:
                new_streams.append(stream)
                continue
            x1 = _mix_out(stream, mod, mod_row, [yf, yb, bonf, bonb, gate, of, ob, z_g, s5f, s5b, z_s5],
                          mix_params)
            final = (not is_ctx) and l == depth - 1
            new_streams.append(_ffn(x1, mod, mod_row, row(norm2_g[l]), w_gu[l], w_dn[l], row(final_g), final))
        ctx, x = new_streams
    return x
```

```python
import functools

import jax
import jax.numpy as jnp
from jax import lax
from jax.experimental import pallas as pl
from jax.experimental.pallas import tpu as pltpu

F32 = jnp.float32
MXU_DTYPE = jnp.bfloat16
HIGHEST = lax.Precision.HIGHEST

NORM_EPS = 1e-6
LNX_EPS = 64e-5
HEAD_DIM = 64
GRID_W = 64
CHUNK = 64
CHUNKS_PER_STEP = 2
SUBLANES = 8
LANES = 128
RW_DECAY_LORA = 64
RW_ICLR_LORA = 64
RW_GATE_LORA = 128
S5_CH = 16
S5_STATE = 64
VMEM_LIMIT = 56 * 1024 * 1024


def _cparams(*sem):
    return pltpu.CompilerParams(dimension_semantics=sem, vmem_limit_bytes=VMEM_LIMIT)


def _sigmoid(x):
    return 1.0 / (1.0 + jnp.exp(-x))


def _silu(x):
    return x * _sigmoid(x)


def _softplus(x):
    return jnp.maximum(x, 0.0) + jnp.log(1.0 + jnp.exp(-jnp.abs(x)))


def _gelu_tanh(x):
    return 0.5 * x * (1.0 + jnp.tanh(0.7978845608028654 * (x + 0.044715 * x * x * x)))


def _mm(a, b):
    return jnp.dot(a.astype(MXU_DTYPE), b.astype(MXU_DTYPE), preferred_element_type=F32)


def _mm_nt(a, b):
    return lax.dot_general(a.astype(MXU_DTYPE), b.astype(MXU_DTYPE), (((1,), (1,)), ((), ())),
                           preferred_element_type=F32)


def _mm_tn(a, b):
    return lax.dot_general(a.astype(MXU_DTYPE), b.astype(MXU_DTYPE), (((0,), (0,)), ((), ())),
                           preferred_element_type=F32)


def _mm_hi(a, b):
    return jnp.dot(a, b, precision=HIGHEST, preferred_element_type=F32)


def _mm_nt_hi(a, b):
    return lax.dot_general(a, b, (((1,), (1,)), ((), ())), precision=HIGHEST,
                           preferred_element_type=F32)


def _head_mean_matrix(width):
    r = lax.broadcasted_iota(jnp.int32, (width, width), 0) // HEAD_DIM
    c = lax.broadcasted_iota(jnp.int32, (width, width), 1) // HEAD_DIM
    return jnp.where(r == c, 1.0 / HEAD_DIM, 0.0).astype(F32)


def _tri_masks(n, reverse):
    t = lax.broadcasted_iota(jnp.int32, (n, n), 0)
    j = lax.broadcasted_iota(jnp.int32, (n, n), 1)
    if reverse:
        return j >= t, j > t, t == j
    return j <= t, j < t, t == j


def _shifted(z, prev_row, next_row):
    n = z.shape[0]
    row = lax.broadcasted_iota(jnp.int32, z.shape, 0)
    zp = jnp.where(row == 0, prev_row, pltpu.roll(z, 1, axis=0))
    zn = jnp.where(row == n - 1, next_row, pltpu.roll(z, n - 1, axis=0))
    return zp, zn


def _block_tri_masks(n, reverse):
    t = lax.broadcasted_iota(jnp.int32, (n, n), 0)
    j = lax.broadcasted_iota(jnp.int32, (n, n), 1)
    order = (j >= t) if reverse else (j <= t)
    return order & (t // CHUNK == j // CHUNK)


def _unit_triangular_inverses(nmats, eye):
    c = nmats[0].shape[0]
    t = lax.broadcasted_iota(jnp.int32, (c, c), 0)
    j = lax.broadcasted_iota(jnp.int32, (c, c), 1)
    s = SUBLANES
    same = t // s == j // s
    ps = [jnp.where(same, m, 0.0) for m in nmats]
    xs = [eye + p for p in ps]
    for _ in range(2):
        ps = [_mm(p, p) for p in ps]
        xs = [x + _mm(p, x) for p, x in zip(ps, xs)]
    while s < c:
        sel = (t // (2 * s) == j // (2 * s)) & (t // s != j // s)
        cx = [_mm(jnp.where(sel, m, 0.0), x) for m, x in zip(nmats, xs)]
        xs = [x + _mm(x, y) for x, y in zip(xs, cx)]
        s *= 2
    return xs


def _scan_state_free_stage(items, eye):
    n = items[0]['v'].shape[1]
    xs = _unit_triangular_inverses([it['a_ab'] for it in items], eye)
    akv = [_mm(it['a_ak'], it['v']) for it in items]
    ws = [_mm(x, jnp.concatenate([it['at'], a], axis=1)) for x, it, a in zip(xs, items, akv)]
    y0 = [_mm(it['a_rk'], it['v']) for it in items]
    kv = [_mm_tn(it['v'], it['kg']) for it in items]
    for it, w, y, m in zip(items, ws, y0, kv):
        it['lhs'] = jnp.concatenate([w[:, :n], it['rt']], axis=0)
        it['p2'] = w[:, n:]
        it['y0'] = y
        it['kv'] = m


def _scan_state_stage(items, states):
    c = items[0]['p2'].shape[0]
    ps = [_mm_nt(it['lhs'], s) for it, s in zip(items, states)]
    us = [p[:c] + it['p2'] for p, it in zip(ps, items)]
    ys = [p[c:] + _mm(it['a_rb'], u) + it['y0'] for p, it, u in zip(ps, items, us)]
    new = [s * it['dec'] + _mm_tn(u, it['bg']) + it['kv'] for s, it, u in zip(states, items, us)]
    return ys, new


def _run_chains(items, heads, nb, s_scr, eye):
    _scan_state_free_stage(list(items.values()), eye)
    chains = [(d, h) for d in range(2) for h in range(heads)]
    states = [s_scr[d, h] for d, h in chains]
    ys = {}
    for step in range(nb):
        keys = [(d, h, nb - 1 - step if d else step) for d, h in chains]
        out, states = _scan_state_stage([items[k] for k in keys], states)
        ys.update(zip(keys, out))
    for (d, h), s in zip(chains, states):
        s_scr[d, h] = s
    return [jnp.concatenate([jnp.concatenate([ys[(d, h, j)] for h in range(heads)], axis=1)
                             for j in range(nb)], axis=0) for d in range(2)]


def _dplr_items(d, heads, nb, r, lw, k, v, a, b):
    n, w = r.shape
    incl, strict, _ = _tri_masks(CHUNK, d == 1)
    g = _mm_hi(_block_tri_masks(n, d == 1).astype(F32), lw)
    gts = [jnp.sum(lw[j * CHUNK:(j + 1) * CHUNK], axis=0, keepdims=True) for j in range(nb)]
    gtb = jnp.concatenate([jnp.broadcast_to(gt, (CHUNK, w)) for gt in gts], axis=0)
    e_neg = jnp.exp(-g)
    e_rem = jnp.exp(gtb - g)
    rt = r * jnp.exp(g)
    at = a * jnp.exp(g - lw)
    lhs = (at, rt)
    rhs = (b * e_neg, k * e_neg)
    bg = b * e_rem
    kg = k * e_rem
    items = {}
    for h in range(heads):
        for j in range(nb):
            rows = slice(j * CHUNK, (j + 1) * CHUNK)
            cols = slice(h * HEAD_DIM, (h + 1) * HEAD_DIM)
            am = _mm_nt(jnp.concatenate([t[rows, cols] for t in lhs], axis=0),
                        jnp.concatenate([t[rows, cols] for t in rhs], axis=0))
            items[(d, h, j)] = dict(
                at=at[rows, cols], rt=rt[rows, cols], v=v[rows, cols], bg=bg[rows, cols], kg=kg[rows, cols],
                dec=jnp.exp(gts[j][:, cols]),
                a_ab=jnp.where(strict, am[:CHUNK, :CHUNK], 0.0), a_ak=jnp.where(strict, am[:CHUNK, CHUNK:], 0.0),
                a_rb=jnp.where(incl, am[CHUNK:, :CHUNK], 0.0), a_rk=jnp.where(incl, am[CHUNK:, CHUNK:], 0.0))
    return items


def _gdn_items(d, heads, nb, q, k, v, beta_all, g_all, lg_all, rows_t):
    n = q.shape[0]
    incl, strict, _ = _tri_masks(CHUNK, d == 1)

    def dec(col, row, mask):
        return jnp.where(mask, jnp.exp(jnp.minimum(col - row, 0.0)), 0.0)

    items = {}
    for h in range(heads):
        cb = d * heads + h
        cg = 2 * heads + cb
        for j in range(nb):
            rows = slice(j * CHUNK, (j + 1) * CHUNK)
            cols = slice(h * HEAD_DIM, (h + 1) * HEAD_DIM)
            qh, kh = q[rows, cols], k[rows, cols]
            bk = kh * beta_all[rows, cb:cb + 1]
            g_col, lg_col = g_all[rows, cg:cg + 1], lg_all[rows, cg:cg + 1]
            g_row = rows_t[cg:cg + 1, j * CHUNK:(j + 1) * CHUNK]
            lg_row = rows_t[cg:cg + 1, n + j * CHUNK:n + (j + 1) * CHUNK]
            gx_col = g_col - lg_col
            gx_row = g_row - lg_row
            gram = _mm_nt(jnp.concatenate([kh, qh], axis=0), bk)
            kbk, qbk = gram[:CHUNK], gram[CHUNK:]
            gt = jnp.sum(lg_col, axis=0, keepdims=True)
            items[(d, h, j)] = dict(
                at=kh * jnp.exp(gx_col), rt=qh * jnp.exp(g_col), v=v[rows, cols],
                bg=-bk * jnp.exp(gt - gx_col), kg=bk * jnp.exp(gt - g_col), dec=jnp.exp(gt),
                a_ab=-kbk * dec(gx_col, gx_row, strict), a_ak=kbk * dec(gx_col, g_row, strict),
                a_rb=-qbk * dec(g_col, gx_row, incl), a_rk=qbk * dec(g_col, g_row, incl))
    return items


def _ada_kernel(cond_ref, w_ref, b_ref, o_ref):
    cnd = cond_ref[...]
    o_ref[...] = _mm_hi(_silu(cnd), w_ref[...]) + b_ref[...]


def _ada_modulation(cond, ada_w, ada_b):
    depth, d, n6 = ada_w.shape
    tn = n6 // 4
    return pl.pallas_call(
        _ada_kernel,
        grid=(depth, n6 // tn),
        in_specs=[pl.BlockSpec((SUBLANES, d), lambda l, j: (0, 0)),
                  pl.BlockSpec((None, d, tn), lambda l, j: (l, 0, j)),
                  pl.BlockSpec((None, 1, tn), lambda l, j: (l, 0, j))],
        out_specs=pl.BlockSpec((None, SUBLANES, tn), lambda l, j: (l, 0, j)),
        out_shape=jax.ShapeDtypeStruct((depth, SUBLANES, n6), F32),
        compiler_params=_cparams("parallel", "parallel"),
    )(cond, ada_w, ada_b.reshape(depth, 1, n6))


def _in_kernel(splits, x_ref, mod_ref, g_ref, w_ref, *o_refs):
    x = x_ref[...]
    ms = jnp.mean(x * x, axis=-1, keepdims=True)
    h = x * lax.rsqrt(ms + NORM_EPS) * g_ref[...]
    h = h * (1.0 + mod_ref[1:2, :]) + mod_ref[0:1, :]
    z = _mm(h, w_ref[...])
    off = 0
    for o_ref, width in zip(o_refs, splits):
        o_ref[...] = z[:, off:off + width]
        off += width


def _in_projection(xs, mod, mod_row, gain, w_perm, splits):
    bsz, n, d = xs.shape
    tm = min(512, n)
    total = sum(splits)
    return pl.pallas_call(
        functools.partial(_in_kernel, splits),
        grid=(bsz, n // tm),
        in_specs=[pl.BlockSpec((None, tm, d), lambda b, i: (b, i, 0)),
                  pl.BlockSpec((None, 6, d), lambda b, i: (mod_row(b), 0, 0)),
                  pl.BlockSpec((1, d), lambda b, i: (0, 0)),
                  pl.BlockSpec((d, total), lambda b, i: (0, 0))],
        out_specs=[pl.BlockSpec((None, tm, w), lambda b, i: (b, i, 0)) for w in splits],
        out_shape=[jax.ShapeDtypeStruct((bsz, n, w), F32) for w in splits],
        compiler_params=_cparams("parallel", "parallel"),
    )(xs, mod, gain, w_perm)


def _block_rows(n):
    nb = CHUNKS_PER_STEP
    while n % (nb * CHUNK):
        nb //= 2
    return nb * CHUNK


def _chunk_specs(width, nc, rows):
    cpb = rows // SUBLANES
    last_tile = nc * cpb - 1
    specs = []
    for rev in (False, True):
        ci = (lambda i: nc - 1 - i) if rev else (lambda i: i)
        specs.append(pl.BlockSpec((None, rows, width), lambda b, i, ci=ci: (b, ci(i), 0)))
        specs.append(pl.BlockSpec((None, SUBLANES, width),
                                  lambda b, i, ci=ci: (b, jnp.maximum(ci(i) * cpb - 1, 0), 0)))
        specs.append(pl.BlockSpec((None, SUBLANES, width),
                                  lambda b, i, ci=ci: (b, jnp.minimum((ci(i) + 1) * cpb, last_tile), 0)))
    return specs


def _halo_rows(p_ref, n_ref, ci, nc):
    prev_row = jnp.where(ci > 0, p_ref[SUBLANES - 1:SUBLANES, :], 0.0)
    next_row = jnp.where(ci < nc - 1, n_ref[0:1, :], 0.0)
    return prev_row, next_row


def _rw_kernel(nc, heads,
               zf_ref, zfp_ref, zfn_ref, zb_ref, zbp_ref, zbn_ref,
               mu_ref, w0_ref, wup_ref, a0_ref, aup_ref, gup_ref, kk_ref, ka_ref, rk_ref, s0_ref,
               yf_ref, yb_ref, bonf_ref, bonb_ref, gate_ref, sfin_ref, s_scr):
    i = pl.program_id(1)
    w = heads * HEAD_DIM

    @pl.when(i == 0)
    def _():
        s_scr[...] = s0_ref[...]

    hmean = _head_mean_matrix(w)
    nb = zf_ref.shape[0] // CHUNK
    items = {}
    views = ((zf_ref, zfp_ref, zfn_ref, i, bonf_ref),
             (zb_ref, zbp_ref, zbn_ref, nc - 1 - i, bonb_ref))
    for d, (z_ref, p_ref, n_ref, ci, bon_ref) in enumerate(views):
        z = z_ref[...]
        zp, zn = _shifted(z, *_halo_rows(p_ref, n_ref, ci, nc))
        z = z + (0.5 * (zp + zn) - z) * mu_ref[...]
        r = z[:, :w]
        k = z[:, w:2 * w]
        v = z[:, 2 * w:3 * w]
        o = 3 * w
        wd = z[:, o:o + RW_DECAY_LORA]
        ad = z[:, o + RW_DECAY_LORA:o + RW_DECAY_LORA + RW_ICLR_LORA]
        gd = z[:, o + RW_DECAY_LORA + RW_ICLR_LORA:]
        if d == 0:
            gate_ref[...] = _mm(_sigmoid(gd), gup_ref[...])
        kk = k * kk_ref[...]
        kk = kk * lax.rsqrt(_mm_hi(kk * kk, hmean) * HEAD_DIM + NORM_EPS)
        w_log = -_softplus(-(w0_ref[d:d + 1, :] + _mm(jnp.tanh(wd), wup_ref[d]))) - 0.5
        lw = -jnp.exp(w_log)
        a = _sigmoid(a0_ref[d:d + 1, :] + _mm(ad, aup_ref[d]))
        kd = k * (1.0 + (a - 1.0) * ka_ref[...])
        bon_ref[...] = _mm_hi(r * kd * rk_ref[...], hmean) * HEAD_DIM * v
        items.update(_dplr_items(d, heads, nb, r, lw, kd, v, -kk, kk * a))
    eye = _tri_masks(CHUNK, False)[2].astype(F32)
    yf_ref[...], yb_ref[...] = _run_chains(items, heads, nb, s_scr, eye)

    @pl.when(i == nc - 1)
    def _():
        sfin_ref[...] = s_scr[...]


def _rwkv_scan(z_rw, p, s0):
    bsz, n, zw = z_rw.shape
    rows = _block_rows(n)
    nc = n // rows
    w = p['rw_kk'].shape[-1]
    heads = w // HEAD_DIM
    full = lambda a: pl.BlockSpec(a.shape, lambda b, i, nd=a.ndim: (0,) * nd)
    params = [p['rw_mu'], p['rw_w0'], p['rw_wup'], p['rw_a0'], p['rw_aup'], p['rw_gup'],
              p['rw_kk'], p['rw_ka'], p['rw_rk']]
    state_spec = pl.BlockSpec((2, None, heads, HEAD_DIM, HEAD_DIM), lambda b, i: (0, b, 0, 0, 0))
    fwd = pl.BlockSpec((None, rows, w), lambda b, i: (b, i, 0))
    bwd = pl.BlockSpec((None, rows, w), lambda b, i: (b, nc - 1 - i, 0))
    tok = jax.ShapeDtypeStruct((bsz, n, w), F32)
    return pl.pallas_call(
        functools.partial(_rw_kernel, nc, heads),
        grid=(bsz, nc),
        in_specs=_chunk_specs(zw, nc, rows) + [full(a) for a in params] + [state_spec],
        out_specs=[fwd, bwd, fwd, bwd, fwd, state_spec],
        out_shape=[tok, tok, tok, tok, tok, jax.ShapeDtypeStruct(s0.shape, F32)],
        scratch_shapes=[pltpu.VMEM((2, heads, HEAD_DIM, HEAD_DIM), F32)],
        compiler_params=_cparams("parallel", "arbitrary"),
    )(z_rw, z_rw, z_rw, z_rw, z_rw, z_rw, *params, s0)


def _gd_kernel(nc, heads,
               zf_ref, zfp_ref, zfn_ref, zb_ref, zbp_ref, zbn_ref, baf_ref, bab_ref,
               conv_ref, alog_ref, dtb_ref, s0_ref,
               of_ref, ob_ref, sfin_ref, s_scr):
    i = pl.program_id(1)
    w = heads * HEAD_DIM

    @pl.when(i == 0)
    def _():
        s_scr[...] = s0_ref[...]

    hmean = _head_mean_matrix(w)
    lane_eye = _tri_masks(LANES, False)[2].astype(F32)
    nb = zf_ref.shape[0] // CHUNK
    items = {}
    views = ((zf_ref, zfp_ref, zfn_ref, baf_ref, i),
             (zb_ref, zbp_ref, zbn_ref, bab_ref, nc - 1 - i))
    for d, (z_ref, p_ref, n_ref, ba_ref, ci) in enumerate(views):
        z = z_ref[...]
        zp, zn = _shifted(z, *_halo_rows(p_ref, n_ref, ci, nc))
        x = _silu(conv_ref[0:1, :] * zp + conv_ref[1:2, :] * z + conv_ref[2:3, :] * zn)
        q = x[:, :w]
        k = x[:, w:2 * w]
        v = x[:, 2 * w:]
        q = q * lax.rsqrt(_mm_hi(q * q, hmean) * HEAD_DIM + NORM_EPS) * (HEAD_DIM ** -0.5)
        k = k * lax.rsqrt(_mm_hi(k * k, hmean) * HEAD_DIM + NORM_EPS)
        ba = ba_ref[...]
        beta_all = _sigmoid(ba)
        lg_all = -jnp.exp(alog_ref[...]) * _softplus(ba + dtb_ref[...])
        g_all = _mm_hi(_block_tri_masks(z.shape[0], d == 1).astype(F32), lg_all)
        rows_t = _mm_nt_hi(lane_eye, jnp.concatenate([g_all, lg_all], axis=0))
        items.update(_gdn_items(d, heads, nb, q, k, v, beta_all, g_all, lg_all, rows_t))
    eye = _tri_masks(CHUNK, False)[2].astype(F32)
    of_ref[...], ob_ref[...] = _run_chains(items, heads, nb, s_scr, eye)

    @pl.when(i == nc - 1)
    def _():
        sfin_ref[...] = s_scr[...]


def _gdn_scan(z_q, z_ba, p, s0):
    bsz, n, zw = z_q.shape
    rows = _block_rows(n)
    nc = n // rows
    w = zw // 3
    heads = w // HEAD_DIM
    full = lambda a: pl.BlockSpec(a.shape, lambda b, i, nd=a.ndim: (0,) * nd)
    pad = lambda t: jnp.pad(t.reshape(1, 2 * heads), ((0, 0), (2 * heads, LANES - 4 * heads)))
    params = [p['gd_conv'], pad(p['gd_a_log']), pad(p['gd_dt_bias'])]
    state_spec = pl.BlockSpec((2, None, heads, HEAD_DIM, HEAD_DIM), lambda b, i: (0, b, 0, 0, 0))
    fwd = pl.BlockSpec((None, rows, w), lambda b, i: (b, i, 0))
    bwd = pl.BlockSpec((None, rows, w), lambda b, i: (b, nc - 1 - i, 0))
    ba_f = pl.BlockSpec((None, rows, LANES), lambda b, i: (b, i, 0))
    ba_b = pl.BlockSpec((None, rows, LANES), lambda b, i: (b, nc - 1 - i, 0))
    tok = jax.ShapeDtypeStruct((bsz, n, w), F32)
    return pl.pallas_call(
        functools.partial(_gd_kernel, nc, heads),
        grid=(bsz, nc),
        in_specs=_chunk_specs(zw, nc, rows) + [ba_f, ba_b] + [full(a) for a in params]
        + [state_spec],
        out_specs=[fwd, bwd, state_spec],
        out_shape=[tok, tok, jax.ShapeDtypeStruct(s0.shape, F32)],
        scratch_shapes=[pltpu.VMEM((2, heads, HEAD_DIM, HEAD_DIM), F32)],
        compiler_params=_cparams("parallel", "arbitrary"),
    )(z_q, z_q, z_q, z_q, z_q, z_q, z_ba, z_ba, *params, s0)


def _s5_disc_kernel(lre_ref, lim_ref, ldt_ref, bre_ref, bim_ref, abr_ref, abi_ref, bbr_ref, bbi_ref):
    lre = lre_ref[...]
    lim = lim_ref[...]
    dt = jnp.exp(ldt_ref[...])
    mag = jnp.exp(lre * dt)
    abr = mag * jnp.cos(lim * dt)
    abi = mag * jnp.sin(lim * dt)
    den = lre * lre + lim * lim
    fr = ((abr - 1.0) * lre + abi * lim) / den
    fi = (abi * lre - (abr - 1.0) * lim) / den
    abr_ref[...] = abr
    abi_ref[...] = abi
    bbr_ref[...] = fr * bre_ref[...] - fi * bim_ref[...]
    bbi_ref[...] = fr * bim_ref[...] + fi * bre_ref[...]


def _s5_discretise(p):
    _, groups, state = p['s5_lam_re'].shape
    ch = p['s5_b_re'].shape[-1]
    rows = 2 * groups * ch
    spread = lambda t: jnp.broadcast_to(t[:, :, None, :], (2, groups, ch, state)).reshape(rows, state)
    lre = spread(p['s5_lam_re'])
    lim = spread(p['s5_lam_im'])
    ldt = spread(jnp.broadcast_to(p['s5_log_dt'][:, :, None], (2, groups, state)))
    bre = jnp.swapaxes(p['s5_b_re'], 2, 3).reshape(rows, state)
    bim = jnp.swapaxes(p['s5_b_im'], 2, 3).reshape(rows, state)
    out = jax.ShapeDtypeStruct((rows, state), F32)
    abr, abi, bbr, bbi = pl.pallas_call(_s5_disc_kernel, out_shape=[out] * 4)(lre, lim, ldt, bre, bim)
    eye = jnp.eye(groups, dtype=F32)
    chain = lambda t: jnp.repeat(t.reshape(2, groups, ch, state)[:, :, 0, :].reshape(2, groups * state),
                                 SUBLANES // 2, axis=0)

    def bmat(t):
        t = t.reshape(2, groups, ch, state)
        return (t[:, :, :, None, :] * eye[None, :, None, :, None]).reshape(2, groups * ch, groups * state)

    def cmat(t):
        t = jnp.swapaxes(t, 2, 3)
        return (t[:, :, :, None, :] * eye[None, :, None, :, None]).reshape(2, groups * state, groups * ch)

    b_all = jnp.concatenate([bmat(bbr), bmat(bbi)], axis=-1).astype(MXU_DTYPE)
    c_all = jnp.stack([cmat(p['s5_c_re']), cmat(p['s5_c_im'])], axis=1).astype(MXU_DTYPE)
    return chain(abr), chain(abi), b_all, c_all


def _s5_kernel(tc, u_ref, b_ref, c_ref, ar_ref, ai_ref, h0r_ref, h0i_ref,
               y_ref, hfr_ref, hfi_ref, bur, bui, hr, hi):
    i = pl.program_id(0)
    rows = tc * SUBLANES
    ns = ar_ref.shape[1]

    @pl.when(i == 0)
    def _():
        hr[...] = h0r_ref[...]
        hi[...] = h0i_ref[...]

    u = u_ref[...].reshape(rows, u_ref.shape[-1])
    fwd_row = (lax.broadcasted_iota(jnp.int32, (rows, 1), 0) % SUBLANES) < (SUBLANES // 2)
    bu = jnp.where(fwd_row, _mm(u, b_ref[0]), _mm(u, b_ref[1]))
    bur[...] = bu[:, :ns]
    bui[...] = bu[:, ns:]
    cw = 512
    for cb in range(ns // cw):
        cols = slice(cb * cw, (cb + 1) * cw)
        ar = ar_ref[:, cols]
        ai = ai_ref[:, cols]

        def body(t, carry, cols=cols, ar=ar, ai=ai):
            h_r, h_i = carry
            o = pl.multiple_of(t * SUBLANES, SUBLANES)
            n_r = ar * h_r - ai * h_i + bur[pl.ds(o, SUBLANES), cols]
            n_i = ar * h_i + ai * h_r + bui[pl.ds(o, SUBLANES), cols]
            bur[pl.ds(o, SUBLANES), cols] = n_r
            bui[pl.ds(o, SUBLANES), cols] = n_i
            return n_r, n_i

        h_r, h_i = lax.fori_loop(0, tc, body, (hr[:, cols], hi[:, cols]), unroll=8)
        hr[:, cols] = h_r
        hi[:, cols] = h_i
    h_re = bur[...]
    h_im = bui[...]
    y = jnp.where(fwd_row,
                  _mm(h_re, c_ref[0, 0]) - _mm(h_im, c_ref[0, 1]),
                  _mm(h_re, c_ref[1, 0]) - _mm(h_im, c_ref[1, 1]))
    y_ref[...] = y.reshape(y_ref.shape)
    hfr_ref[...] = hr[...]
    hfi_ref[...] = hi[...]


def _s5_scan(u2, disc, h0):
    ar, ai, b_all, c_all = disc
    n, chains, w = u2.shape
    ns = ar.shape[1]
    tc = min(64, n)
    full = lambda a: pl.BlockSpec(a.shape, lambda i, nd=a.ndim: (0,) * nd)
    st = jax.ShapeDtypeStruct((chains, ns), F32)
    y2, hfr, hfi = pl.pallas_call(
        functools.partial(_s5_kernel, tc),
        grid=(n // tc,),
        in_specs=[pl.BlockSpec((tc, chains, w), lambda i: (i, 0, 0)),
                  full(b_all), full(c_all), full(ar), full(ai), full(h0[0]), full(h0[1])],
        out_specs=[pl.BlockSpec((tc, chains, w), lambda i: (i, 0, 0)), full(ar), full(ar)],
        out_shape=[jax.ShapeDtypeStruct((n, chains, w), F32), st, st],
        scratch_shapes=[pltpu.VMEM((tc * chains, ns), F32), pltpu.VMEM((tc * chains, ns), F32),
                        pltpu.VMEM((chains, ns), F32), pltpu.VMEM((chains, ns), F32)],
        compiler_params=_cparams("arbitrary"),
    )(u2, b_all, c_all, ar, ai, h0[0], h0[1])
    return y2, (hfr, hfi)


def _to_scan_order(u, column_major):
    bsz, n, w = u.shape
    if column_major:
        return u.reshape(bsz, n // GRID_W, GRID_W, w).transpose(2, 1, 0, 3).reshape(n, bsz, w)
    return u.transpose(1, 0, 2)


def _from_scan_order(y, column_major):
    n, bsz, w = y.shape
    if column_major:
        return y.reshape(GRID_W, n // GRID_W, bsz, w).transpose(2, 1, 0, 3).reshape(bsz, n, w)
    return y.transpose(1, 0, 2)


def _mixout_kernel(x_ref, mod_ref, yf_ref, yb_ref, bonf_ref, bonb_ref, gate_ref, of_ref, ob_ref, zg_ref,
                   s5f_ref, s5b_ref, u5_ref, lnxg_ref, lnxb_ref, gdg_ref, s5d_ref, gluw_ref, glub_ref,
                   wout_ref, o_ref):
    hmean = _head_mean_matrix(yf_ref.shape[-1])
    y = yf_ref[...] + yb_ref[...]
    yc = y - _mm_hi(y, hmean)
    yn = yc * lax.rsqrt(_mm_hi(yc * yc, hmean) + LNX_EPS)
    y_rw = (yn * lnxg_ref[...] + lnxb_ref[...] + bonf_ref[...] + bonb_ref[...]) * gate_ref[...]
    o = of_ref[...] + ob_ref[...]
    o = o * lax.rsqrt(_mm_hi(o * o, hmean) + NORM_EPS) * gdg_ref[...]
    y_gd = o * _silu(zg_ref[...])
    y5 = _gelu_tanh(s5f_ref[...] + s5b_ref[...] + s5d_ref[...] * u5_ref[...])
    y5 = y5 * _sigmoid(_mm(y5, gluw_ref[...]) + glub_ref[...])
    mix = jnp.concatenate([y_rw, y_gd, y5], axis=1)
    o_ref[...] = x_ref[...] + mod_ref[2:3, :] * _mm(mix, wout_ref[...])


def _mix_out(xs, mod, mod_row, toks, params):
    bsz, n, d = xs.shape
    tm = min(512, n)
    tok_spec = lambda a: pl.BlockSpec((None, tm, a.shape[-1]), lambda b, i: (b, i, 0))
    full = lambda a: pl.BlockSpec(a.shape, lambda b, i, nd=a.ndim: (0,) * nd)
    return pl.pallas_call(
        _mixout_kernel,
        grid=(bsz, n // tm),
        in_specs=[tok_spec(xs), pl.BlockSpec((None, 6, d), lambda b, i: (mod_row(b), 0, 0))]
        + [tok_spec(a) for a in toks] + [full(a) for a in params],
        out_specs=tok_spec(xs),
        out_shape=jax.ShapeDtypeStruct(xs.shape, F32),
        compiler_params=_cparams("parallel", "parallel"),
    )(xs, mod, *toks, *params)


def _ffn_kernel(hidden, final, x_ref, mod_ref, g_ref, wgu_ref, wd_ref, fg_ref, o_ref):
    x = x_ref[...]
    ms = jnp.mean(x * x, axis=-1, keepdims=True)
    h = x * lax.rsqrt(ms + NORM_EPS) * g_ref[...]
    h = h * (1.0 + mod_ref[4:5, :]) + mod_ref[3:4, :]
    gu = _mm(h, wgu_ref[...])
    act = _silu(gu[:, :hidden]) * gu[:, hidden:]
    o = x + mod_ref[5:6, :] * _mm(act, wd_ref[...])
    if final:
        ms = jnp.mean(o * o, axis=-1, keepdims=True)
        o = o * lax.rsqrt(ms + NORM_EPS) * fg_ref[...]
    o_ref[...] = o


def _ffn(xs, mod, mod_row, gain, w_gu, w_down, final_gain, final):
    bsz, n, d = xs.shape
    hidden = w_down.shape[0]
    tm = min(256, n)
    const = lambda a: pl.BlockSpec(a.shape, lambda b, i, nd=a.ndim: (0,) * nd,
                                   pipeline_mode=pl.Buffered(1))
    tok = pl.BlockSpec((None, tm, d), lambda b, i: (b, i, 0))
    return pl.pallas_call(
        functools.partial(_ffn_kernel, hidden, final),
        grid=(bsz, n // tm),
        in_specs=[tok, pl.BlockSpec((None, 6, d), lambda b, i: (mod_row(b), 0, 0)),
                  const(gain), const(w_gu), const(w_down), const(final_gain)],
        out_specs=tok,
        out_shape=jax.ShapeDtypeStruct(xs.shape, F32),
        compiler_params=_cparams("parallel", "parallel"),
    )(xs, mod, gain, w_gu, w_down, final_gain)


def kernel(x, c, ctx, c_ctx, norm1_g, norm2_g, final_g, ada_w, ada_b, w_in, w_out,
           rw_mu, rw_w0, rw_wup, rw_a0, rw_aup, rw_gup, rw_kk, rw_ka, rw_rk, rw_lnx_g, rw_lnx_b,
           gd_conv, gd_a_log, gd_dt_bias, gd_norm_g,
           s5_lam_re, s5_lam_im, s5_log_dt, s5_b_re, s5_b_im, s5_c_re, s5_c_im, s5_d, s5_glu_w, s5_glu_b,
           ffn_w_gate, ffn_w_up, ffn_w_down):
    bsz, n, d = x.shape
    depth = w_in.shape[0]
    rw_w = rw_kk.shape[-1]
    gd_w = gd_conv.shape[-1] // 3
    s5_w = s5_d.shape[-1]
    rw_heads = rw_w // HEAD_DIM
    gd_heads = gd_w // HEAD_DIM
    in_rw = rw_mu.shape[-1]
    assert 2 * bsz == SUBLANES and n % GRID_W == 0 and n % CHUNK == 0 and ctx.shape[1] % CHUNK == 0

    cond = jnp.concatenate([c, c_ctx[None, :], jnp.zeros((SUBLANES - bsz - 1, d), F32)], axis=0)
    mod_all = _ada_modulation(cond, ada_w, ada_b).reshape(depth, SUBLANES, 6, d)
    x_row = lambda b: b
    ctx_row = lambda b: bsz

    o_gd = in_rw
    o_ba = o_gd + 3 * gd_w
    o_gate = o_ba + 4 * gd_heads
    o_s5 = o_gate + gd_w
    splits = (in_rw, 3 * gd_w, gd_w, s5_w, LANES)
    w_perm = jnp.concatenate(
        [w_in[:, :, :o_gd], w_in[:, :, o_gd:o_ba], w_in[:, :, o_gate:o_s5], w_in[:, :, o_s5:],
         w_in[:, :, o_ba:o_gate], jnp.zeros((depth, d, LANES - 4 * gd_heads), F32)], axis=-1).astype(MXU_DTYPE)
    w_gu = jnp.concatenate([ffn_w_gate, ffn_w_up], axis=-1).astype(MXU_DTYPE)
    w_dn = ffn_w_down.astype(MXU_DTYPE)
    w_o = w_out.astype(MXU_DTYPE)
    glu_w = s5_glu_w.astype(MXU_DTYPE)
    row = lambda t: t.reshape(1, -1)

    for l in range(depth):
        p = {'rw_mu': row(rw_mu[l]), 'rw_w0': rw_w0[l], 'rw_wup': rw_wup[l], 'rw_a0': rw_a0[l],
             'rw_aup': rw_aup[l], 'rw_gup': rw_gup[l], 'rw_kk': row(rw_kk[l]), 'rw_ka': row(rw_ka[l]),
             'rw_rk': row(rw_rk[l]), 'gd_conv': gd_conv[l], 'gd_a_log': gd_a_log[l],
             'gd_dt_bias': gd_dt_bias[l], 's5_lam_re': s5_lam_re[l], 's5_lam_im': s5_lam_im[l],
             's5_log_dt': s5_log_dt[l], 's5_b_re': s5_b_re[l], 's5_b_im': s5_b_im[l],
             's5_c_re': s5_c_re[l], 's5_c_im': s5_c_im[l]}
        mix_params = [row(rw_lnx_g[l]), row(rw_lnx_b[l]), row(jnp.tile(gd_norm_g[l], gd_heads)),
                      row(s5_d[l]), glu_w[l], row(s5_glu_b[l]), w_o[l]]
        disc = _s5_discretise(p)
        mod = mod_all[l]
        states = (jnp.zeros((2, bsz, rw_heads, HEAD_DIM, HEAD_DIM), F32),
                  jnp.zeros((2, bsz, gd_heads, HEAD_DIM, HEAD_DIM), F32),
                  (jnp.zeros((SUBLANES, disc[0].shape[1]), F32),) * 2)
        new_streams = []
        for stream, mod_row, column_major, is_ctx in ((ctx, ctx_row, False, True), (x, x_row, True, False)):
            z_rw, z_q, z_g, z_s5, z_ba = _in_projection(stream, mod, mod_row, row(norm1_g[l]), w_perm[l], splits)
            yf, yb, bonf, bonb, gate, s_rw = _rwkv_scan(z_rw, p, states[0])
            of, ob, s_gd = _gdn_scan(z_q, z_ba, p, states[1])
            ut = _to_scan_order(z_s5, column_major)
            y2, s_s5 = _s5_scan(jnp.concatenate([ut, ut[::-1]], axis=1), disc, states[2])
            s5f = _from_scan_order(y2[:, :bsz], column_major)
            s5b = _from_scan_order(y2[::-1, bsz:], column_major)
            states = (s_rw, s_gd, s_s5)
            if is_ctx and l == depth - 1:
                new_streams.append(stream)
                continue
            x1 = _mix_out(stream, mod, mod_row, [yf, yb, bonf, bonb, gate, of, ob, z_g, s5f, s5b, z_s5],
                          mix_params)
            final = (not is_ctx) and l == depth - 1
            new_streams.append(_ffn(x1, mod, mod_row, row(norm2_g[l]), w_gu[l], w_dn[l], row(final_g), final))
        ctx, x = new_streams
    return x
```

```python
import functools

import jax
import jax.numpy as jnp
from jax import lax
from jax.experimental import pallas as pl
from jax.experimental.pallas import tpu as pltpu

F32 = jnp.float32
MXU_DTYPE = jnp.bfloat16
HIGHEST = lax.Precision.HIGHEST

NORM_EPS = 1e-6
LNX_EPS = 64e-5
HEAD_DIM = 64
GRID_W = 64
CHUNK = 64
CHUNKS_PER_STEP = 2
STAT_TERMS = 2
SUM_TERMS = 3
SUBLANES = 8
LANES = 128
RW_DECAY_LORA = 64
RW_ICLR_LORA = 64
RW_GATE_LORA = 128
S5_CH = 16
S5_STATE = 64
VMEM_LIMIT = 56 * 1024 * 1024


def _cparams(*sem):
    return pltpu.CompilerParams(dimension_semantics=sem, vmem_limit_bytes=VMEM_LIMIT)


def _sigmoid(x):
    return 1.0 / (1.0 + jnp.exp(-x))


def _silu(x):
    return x * _sigmoid(x)


def _softplus(x):
    return jnp.maximum(x, 0.0) + jnp.log(1.0 + jnp.exp(-jnp.abs(x)))


def _gelu_tanh(x):
    return 0.5 * x * (1.0 + jnp.tanh(0.7978845608028654 * (x + 0.044715 * x * x * x)))


def _mm(a, b):
    return jnp.dot(a.astype(MXU_DTYPE), b.astype(MXU_DTYPE), preferred_element_type=F32)


def _mm_nt(a, b):
    return lax.dot_general(a.astype(MXU_DTYPE), b.astype(MXU_DTYPE), (((1,), (1,)), ((), ())),
                           preferred_element_type=F32)


def _mm_tn(a, b):
    return lax.dot_general(a.astype(MXU_DTYPE), b.astype(MXU_DTYPE), (((0,), (0,)), ((), ())),
                           preferred_element_type=F32)


def _mm_hi(a, b):
    return jnp.dot(a, b, precision=HIGHEST, preferred_element_type=F32)


def _split(x, terms):
    parts = []
    for _ in range(terms):
        p = x.astype(MXU_DTYPE)
        parts.append(p)
        x = x - p.astype(F32)
    return parts


def _mm_exact_rhs(a, e, terms):
    e = e.astype(MXU_DTYPE)
    return sum(jnp.dot(p, e, preferred_element_type=F32) for p in _split(a, terms))


def _mm_exact_lhs(e, b, terms):
    e = e.astype(MXU_DTYPE)
    return sum(jnp.dot(e, p, preferred_element_type=F32) for p in _split(b, terms))


def _mm_nt_exact_lhs(e, b, terms):
    e = e.astype(MXU_DTYPE)
    return sum(lax.dot_general(e, p, (((1,), (1,)), ((), ())), preferred_element_type=F32)
               for p in _split(b, terms))


def _head_mean_matrix(width):
    r = lax.broadcasted_iota(jnp.int32, (width, width), 0) // HEAD_DIM
    c = lax.broadcasted_iota(jnp.int32, (width, width), 1) // HEAD_DIM
    return jnp.where(r == c, 1.0 / HEAD_DIM, 0.0).astype(F32)


def _tri_masks(n, reverse):
    t = lax.broadcasted_iota(jnp.int32, (n, n), 0)
    j = lax.broadcasted_iota(jnp.int32, (n, n), 1)
    if reverse:
        return j >= t, j > t, t == j
    return j <= t, j < t, t == j


def _shifted(z, prev_row, next_row):
    n = z.shape[0]
    row = lax.broadcasted_iota(jnp.int32, z.shape, 0)
    zp = jnp.where(row == 0, prev_row, pltpu.roll(z, 1, axis=0))
    zn = jnp.where(row == n - 1, next_row, pltpu.roll(z, n - 1, axis=0))
    return zp, zn


def _block_tri_masks(n, reverse):
    t = lax.broadcasted_iota(jnp.int32, (n, n), 0)
    j = lax.broadcasted_iota(jnp.int32, (n, n), 1)
    order = (j >= t) if reverse else (j <= t)
    return order & (t // CHUNK == j // CHUNK)


def _unit_triangular_inverses(nmats, eye):
    c = nmats[0].shape[0]
    t = lax.broadcasted_iota(jnp.int32, (c, c), 0)
    j = lax.broadcasted_iota(jnp.int32, (c, c), 1)
    s = SUBLANES
    same = t // s == j // s
    ps = [jnp.where(same, m, 0.0) for m in nmats]
    xs = [eye + p for p in ps]
    for _ in range(2):
        ps = [_mm(p, p) for p in ps]
        xs = [x + _mm(p, x) for p, x in zip(ps, xs)]
    while s < c:
        sel = (t // (2 * s) == j // (2 * s)) & (t // s != j // s)
        cx = [_mm(jnp.where(sel, m, 0.0), x) for m, x in zip(nmats, xs)]
        xs = [x + _mm(x, y) for x, y in zip(xs, cx)]
        s *= 2
    return xs


def _scan_state_free_stage(items, eye):
    c, n = items[0]['v'].shape
    xs = _unit_triangular_inverses([it['a_ab'] for it in items], eye)
    av = [_mm(jnp.concatenate([it['a_ak'], it['a_rk']], axis=0), it['v']) for it in items]
    ws = [_mm(x, jnp.concatenate([it['at'], a[:c]], axis=1)) for x, it, a in zip(xs, items, av)]
    kv = [_mm_tn(it['v'], it['kg']) for it in items]
    for it, w, a, m in zip(items, ws, av, kv):
        it['lhs'] = jnp.concatenate([w[:, :n], it['rt']], axis=0)
        it['p2'] = w[:, n:]
        it['y0'] = a[c:]
        it['kv'] = m


def _scan_state_stage(items, states):
    c = items[0]['p2'].shape[0]
    ps = [_mm_nt(it['lhs'], s) for it, s in zip(items, states)]
    us = [p[:c] + it['p2'] for p, it in zip(ps, items)]
    ys = [p[c:] + _mm(it['a_rb'], u) + it['y0'] for p, it, u in zip(ps, items, us)]
    new = [s * it['dec'] + _mm_tn(u, it['bg']) + it['kv'] for s, it, u in zip(states, items, us)]
    return ys, new


def _run_chains(items, heads, nb, s_scr, eye):
    _scan_state_free_stage(list(items.values()), eye)
    chains = [(d, h) for d in range(2) for h in range(heads)]
    states = [s_scr[d, h] for d, h in chains]
    ys = {}
    for step in range(nb):
        keys = [(d, h, nb - 1 - step if d else step) for d, h in chains]
        out, states = _scan_state_stage([items[k] for k in keys], states)
        ys.update(zip(keys, out))
    for (d, h), s in zip(chains, states):
        s_scr[d, h] = s
    return [jnp.concatenate([jnp.concatenate([ys[(d, h, j)] for h in range(heads)], axis=1)
                             for j in range(nb)], axis=0) for d in range(2)]


def _dplr_items(d, heads, nb, r, lw, k, v, a, b):
    n, w = r.shape
    incl, strict, _ = _tri_masks(CHUNK, d == 1)
    g = _mm_exact_lhs(_block_tri_masks(n, d == 1), lw, SUM_TERMS)
    gts = [jnp.sum(lw[j * CHUNK:(j + 1) * CHUNK], axis=0, keepdims=True) for j in range(nb)]
    gtb = jnp.concatenate([jnp.broadcast_to(gt, (CHUNK, w)) for gt in gts], axis=0)
    e_neg = jnp.exp(-g)
    e_rem = jnp.exp(gtb - g)
    rt = r * jnp.exp(g)
    at = a * jnp.exp(g - lw)
    lhs = (at, rt)
    rhs = (b * e_neg, k * e_neg)
    bg = b * e_rem
    kg = k * e_rem
    items = {}
    for h in range(heads):
        for j in range(nb):
            rows = slice(j * CHUNK, (j + 1) * CHUNK)
            cols = slice(h * HEAD_DIM, (h + 1) * HEAD_DIM)
            am = _mm_nt(jnp.concatenate([t[rows, cols] for t in lhs], axis=0),
                        jnp.concatenate([t[rows, cols] for t in rhs], axis=0))
            items[(d, h, j)] = dict(
                at=at[rows, cols], rt=rt[rows, cols], v=v[rows, cols], bg=bg[rows, cols], kg=kg[rows, cols],
                dec=jnp.exp(gts[j][:, cols]),
                a_ab=jnp.where(strict, am[:CHUNK, :CHUNK], 0.0), a_ak=jnp.where(strict, am[:CHUNK, CHUNK:], 0.0),
                a_rb=jnp.where(incl, am[CHUNK:, :CHUNK], 0.0), a_rk=jnp.where(incl, am[CHUNK:, CHUNK:], 0.0))
    return items


def _gdn_items(d, heads, nb, q, k, v, beta_all, g_all, lg_all, rows_t):
    n, w = q.shape
    assert CHUNK == HEAD_DIM
    incl, strict, _ = _tri_masks(CHUNK, d == 1)

    def dec(col, row, mask):
        return jnp.where(mask, jnp.exp(jnp.minimum(col - row, 0.0)), 0.0)

    src = lax.broadcasted_iota(jnp.int32, (LANES, w), 0)
    head = lax.broadcasted_iota(jnp.int32, (LANES, w), 1) // HEAD_DIM + d * heads
    beta = _mm_exact_rhs(beta_all, src == head, STAT_TERMS)
    gl = _mm_exact_rhs(jnp.concatenate([g_all, lg_all], axis=0), src == head + 2 * heads, SUM_TERMS)
    g, lg = gl[:n], gl[n:]
    gx = g - lg
    gts = [jnp.sum(lg[j * CHUNK:(j + 1) * CHUNK], axis=0, keepdims=True) for j in range(nb)]
    gtb = jnp.concatenate([jnp.broadcast_to(gt, (CHUNK, w)) for gt in gts], axis=0)
    bk = k * beta
    at = k * jnp.exp(gx)
    rt = q * jnp.exp(g)
    bg = -bk * jnp.exp(gtb - gx)
    kg = bk * jnp.exp(gtb - g)
    items = {}
    for h in range(heads):
        cg = (2 + d) * heads + h
        for j in range(nb):
            rows = slice(j * CHUNK, (j + 1) * CHUNK)
            cols = slice(h * HEAD_DIM, (h + 1) * HEAD_DIM)
            g_col, gx_col = g[rows, cols], gx[rows, cols]
            g_row = rows_t[cg:cg + 1, j * CHUNK:(j + 1) * CHUNK]
            gx_row = g_row - rows_t[cg:cg + 1, n + j * CHUNK:n + (j + 1) * CHUNK]
            gram = _mm_nt(jnp.concatenate([k[rows, cols], q[rows, cols]], axis=0), bk[rows, cols])
            kbk, qbk = gram[:CHUNK], gram[CHUNK:]
            items[(d, h, j)] = dict(
                at=at[rows, cols], rt=rt[rows, cols], v=v[rows, cols], bg=bg[rows, cols], kg=kg[rows, cols],
                dec=jnp.exp(gts[j][:, cols]),
                a_ab=-kbk * dec(gx_col, gx_row, strict), a_ak=kbk * dec(gx_col, g_row, strict),
                a_rb=-qbk * dec(g_col, gx_row, incl), a_rk=qbk * dec(g_col, g_row, incl))
    return items


def _ada_kernel(cond_ref, w_ref, b_ref, o_ref):
    cnd = cond_ref[...]
    o_ref[...] = _mm_hi(_silu(cnd), w_ref[...]) + b_ref[...]


def _ada_modulation(cond, ada_w, ada_b):
    depth, d, n6 = ada_w.shape
    tn = n6 // 4
    return pl.pallas_call(
        _ada_kernel,
        grid=(depth, n6 // tn),
        in_specs=[pl.BlockSpec((SUBLANES, d), lambda l, j: (0, 0)),
                  pl.BlockSpec((None, d, tn), lambda l, j: (l, 0, j)),
                  pl.BlockSpec((None, 1, tn), lambda l, j: (l, 0, j))],
        out_specs=pl.BlockSpec((None, SUBLANES, tn), lambda l, j: (l, 0, j)),
        out_shape=jax.ShapeDtypeStruct((depth, SUBLANES, n6), F32),
        compiler_params=_cparams("parallel", "parallel"),
    )(cond, ada_w, ada_b.reshape(depth, 1, n6))


def _in_kernel(splits, x_ref, mod_ref, g_ref, w_ref, *o_refs):
    x = x_ref[...]
    ms = jnp.mean(x * x, axis=-1, keepdims=True)
    h = x * lax.rsqrt(ms + NORM_EPS) * g_ref[...]
    h = h * (1.0 + mod_ref[1:2, :]) + mod_ref[0:1, :]
    z = _mm(h, w_ref[...])
    off = 0
    for o_ref, width in zip(o_refs, splits):
        o_ref[...] = z[:, off:off + width]
        off += width


def _in_projection(xs, mod, mod_row, gain, w_perm, splits):
    bsz, n, d = xs.shape
    tm = min(512, n)
    total = sum(splits)
    return pl.pallas_call(
        functools.partial(_in_kernel, splits),
        grid=(bsz, n // tm),
        in_specs=[pl.BlockSpec((None, tm, d), lambda b, i: (b, i, 0)),
                  pl.BlockSpec((None, 6, d), lambda b, i: (mod_row(b), 0, 0)),
                  pl.BlockSpec((1, d), lambda b, i: (0, 0)),
                  pl.BlockSpec((d, total), lambda b, i: (0, 0))],
        out_specs=[pl.BlockSpec((None, tm, w), lambda b, i: (b, i, 0)) for w in splits],
        out_shape=[jax.ShapeDtypeStruct((bsz, n, w), F32) for w in splits],
        compiler_params=_cparams("parallel", "parallel"),
    )(xs, mod, gain, w_perm)


def _block_rows(n):
    nb = CHUNKS_PER_STEP
    while n % (nb * CHUNK):
        nb //= 2
    return nb * CHUNK


def _chunk_specs(width, nc, rows):
    cpb = rows // SUBLANES
    last_tile = nc * cpb - 1
    specs = []
    for rev in (False, True):
        ci = (lambda i: nc - 1 - i) if rev else (lambda i: i)
        specs.append(pl.BlockSpec((None, rows, width), lambda b, i, ci=ci: (b, ci(i), 0)))
        specs.append(pl.BlockSpec((None, SUBLANES, width),
                                  lambda b, i, ci=ci: (b, jnp.maximum(ci(i) * cpb - 1, 0), 0)))
        specs.append(pl.BlockSpec((None, SUBLANES, width),
                                  lambda b, i, ci=ci: (b, jnp.minimum((ci(i) + 1) * cpb, last_tile), 0)))
    return specs


def _halo_rows(p_ref, n_ref, ci, nc):
    prev_row = jnp.where(ci > 0, p_ref[SUBLANES - 1:SUBLANES, :], 0.0)
    next_row = jnp.where(ci < nc - 1, n_ref[0:1, :], 0.0)
    return prev_row, next_row


def _rw_kernel(nc, heads,
               zf_ref, zfp_ref, zfn_ref, zb_ref, zbp_ref, zbn_ref,
               mu_ref, w0_ref, wup_ref, a0_ref, aup_ref, gup_ref, kk_ref, ka_ref, rk_ref, s0_ref,
               yf_ref, yb_ref, bonf_ref, bonb_ref, gate_ref, sfin_ref, s_scr):
    i = pl.program_id(1)
    w = heads * HEAD_DIM

    @pl.when(i == 0)
    def _():
        s_scr[...] = s0_ref[...]

    hmean = _head_mean_matrix(w)
    nb = zf_ref.shape[0] // CHUNK
    items = {}
    views = ((zf_ref, zfp_ref, zfn_ref, i, bonf_ref),
             (zb_ref, zbp_ref, zbn_ref, nc - 1 - i, bonb_ref))
    for d, (z_ref, p_ref, n_ref, ci, bon_ref) in enumerate(views):
        z = z_ref[...]
        zp, zn = _shifted(z, *_halo_rows(p_ref, n_ref, ci, nc))
        z = z + (0.5 * (zp + zn) - z) * mu_ref[...]
        r = z[:, :w]
        k = z[:, w:2 * w]
        v = z[:, 2 * w:3 * w]
        o = 3 * w
        wd = z[:, o:o + RW_DECAY_LORA]
        ad = z[:, o + RW_DECAY_LORA:o + RW_DECAY_LORA + RW_ICLR_LORA]
        gd = z[:, o + RW_DECAY_LORA + RW_ICLR_LORA:]
        if d == 0:
            gate_ref[...] = _mm(_sigmoid(gd), gup_ref[...])
        kk = k * kk_ref[...]
        kk = kk * lax.rsqrt(_mm_exact_rhs(kk * kk, hmean, STAT_TERMS) * HEAD_DIM + NORM_EPS)
        w_log = -_softplus(-(w0_ref[d:d + 1, :] + _mm(jnp.tanh(wd), wup_ref[d]))) - 0.5
        lw = -jnp.exp(w_log)
        a = _sigmoid(a0_ref[d:d + 1, :] + _mm(ad, aup_ref[d]))
        kd = k * (1.0 + (a - 1.0) * ka_ref[...])
        bon_ref[...] = _mm_exact_rhs(r * kd * rk_ref[...], hmean, STAT_TERMS) * HEAD_DIM * v
        items.update(_dplr_items(d, heads, nb, r, lw, kd, v, -kk, kk * a))
    eye = _tri_masks(CHUNK, False)[2].astype(F32)
    yf_ref[...], yb_ref[...] = _run_chains(items, heads, nb, s_scr, eye)

    @pl.when(i == nc - 1)
    def _():
        sfin_ref[...] = s_scr[...]


def _rwkv_scan(z_rw, p, s0):
    bsz, n, zw = z_rw.shape
    rows = _block_rows(n)
    nc = n // rows
    w = p['rw_kk'].shape[-1]
    heads = w // HEAD_DIM
    full = lambda a: pl.BlockSpec(a.shape, lambda b, i, nd=a.ndim: (0,) * nd)
    params = [p['rw_mu'], p['rw_w0'], p['rw_wup'], p['rw_a0'], p['rw_aup'], p['rw_gup'],
              p['rw_kk'], p['rw_ka'], p['rw_rk']]
    state_spec = pl.BlockSpec((2, None, heads, HEAD_DIM, HEAD_DIM), lambda b, i: (0, b, 0, 0, 0))
    fwd = pl.BlockSpec((None, rows, w), lambda b, i: (b, i, 0))
    bwd = pl.BlockSpec((None, rows, w), lambda b, i: (b, nc - 1 - i, 0))
    tok = jax.ShapeDtypeStruct((bsz, n, w), F32)
    return pl.pallas_call(
        functools.partial(_rw_kernel, nc, heads),
        grid=(bsz, nc),
        in_specs=_chunk_specs(zw, nc, rows) + [full(a) for a in params] + [state_spec],
        out_specs=[fwd, bwd, fwd, bwd, fwd, state_spec],
        out_shape=[tok, tok, tok, tok, tok, jax.ShapeDtypeStruct(s0.shape, F32)],
        scratch_shapes=[pltpu.VMEM((2, heads, HEAD_DIM, HEAD_DIM), F32)],
        compiler_params=_cparams("parallel", "arbitrary"),
    )(z_rw, z_rw, z_rw, z_rw, z_rw, z_rw, *params, s0)


def _gd_kernel(nc, heads,
               zf_ref, zfp_ref, zfn_ref, zb_ref, zbp_ref, zbn_ref, baf_ref, bab_ref,
               conv_ref, alog_ref, dtb_ref, s0_ref,
               of_ref, ob_ref, sfin_ref, s_scr):
    i = pl.program_id(1)
    w = heads * HEAD_DIM

    @pl.when(i == 0)
    def _():
        s_scr[...] = s0_ref[...]

    hmean = _head_mean_matrix(w)
    lane_eye = _tri_masks(LANES, False)[2].astype(F32)
    nb = zf_ref.shape[0] // CHUNK
    items = {}
    views = ((zf_ref, zfp_ref, zfn_ref, baf_ref, i),
             (zb_ref, zbp_ref, zbn_ref, bab_ref, nc - 1 - i))
    for d, (z_ref, p_ref, n_ref, ba_ref, ci) in enumerate(views):
        z = z_ref[...]
        zp, zn = _shifted(z, *_halo_rows(p_ref, n_ref, ci, nc))
        x = _silu(conv_ref[0:1, :] * zp + conv_ref[1:2, :] * z + conv_ref[2:3, :] * zn)
        q = x[:, :w]
        k = x[:, w:2 * w]
        v = x[:, 2 * w:]
        q = q * lax.rsqrt(_mm_exact_rhs(q * q, hmean, STAT_TERMS) * HEAD_DIM + NORM_EPS) * (HEAD_DIM ** -0.5)
        k = k * lax.rsqrt(_mm_exact_rhs(k * k, hmean, STAT_TERMS) * HEAD_DIM + NORM_EPS)
        ba = ba_ref[...]
        beta_all = _sigmoid(ba)
        lg_all = -jnp.exp(alog_ref[...]) * _softplus(ba + dtb_ref[...])
        g_all = _mm_exact_lhs(_block_tri_masks(z.shape[0], d == 1), lg_all, SUM_TERMS)
        rows_t = _mm_nt_exact_lhs(lane_eye, jnp.concatenate([g_all, lg_all], axis=0), SUM_TERMS)
        items.update(_gdn_items(d, heads, nb, q, k, v, beta_all, g_all, lg_all, rows_t))
    eye = _tri_masks(CHUNK, False)[2].astype(F32)
    of_ref[...], ob_ref[...] = _run_chains(items, heads, nb, s_scr, eye)

    @pl.when(i == nc - 1)
    def _():
        sfin_ref[...] = s_scr[...]


def _gdn_scan(z_q, z_ba, p, s0):
    bsz, n, zw = z_q.shape
    rows = _block_rows(n)
    nc = n // rows
    w = zw // 3
    heads = w // HEAD_DIM
    full = lambda a: pl.BlockSpec(a.shape, lambda b, i, nd=a.ndim: (0,) * nd)
    pad = lambda t: jnp.pad(t.reshape(1, 2 * heads), ((0, 0), (2 * heads, LANES - 4 * heads)))
    params = [p['gd_conv'], pad(p['gd_a_log']), pad(p['gd_dt_bias'])]
    state_spec = pl.BlockSpec((2, None, heads, HEAD_DIM, HEAD_DIM), lambda b, i: (0, b, 0, 0, 0))
    fwd = pl.BlockSpec((None, rows, w), lambda b, i: (b, i, 0))
    bwd = pl.BlockSpec((None, rows, w), lambda b, i: (b, nc - 1 - i, 0))
    ba_f = pl.BlockSpec((None, rows, LANES), lambda b, i: (b, i, 0))
    ba_b = pl.BlockSpec((None, rows, LANES), lambda b, i: (b, nc - 1 - i, 0))
    tok = jax.ShapeDtypeStruct((bsz, n, w), F32)
    return pl.pallas_call(
        functools.partial(_gd_kernel, nc, heads),
        grid=(bsz, nc),
        in_specs=_chunk_specs(zw, nc, rows) + [ba_f, ba_b] + [full(a) for a in params]
        + [state_spec],
        out_specs=[fwd, bwd, state_spec],
        out_shape=[tok, tok, jax.ShapeDtypeStruct(s0.shape, F32)],
        scratch_shapes=[pltpu.VMEM((2, heads, HEAD_DIM, HEAD_DIM), F32)],
        compiler_params=_cparams("parallel", "arbitrary"),
    )(z_q, z_q, z_q, z_q, z_q, z_q, z_ba, z_ba, *params, s0)


def _s5_disc_kernel(lre_ref, lim_ref, ldt_ref, bre_ref, bim_ref, abr_ref, abi_ref, bbr_ref, bbi_ref):
    lre = lre_ref[...]
    lim = lim_ref[...]
    dt = jnp.exp(ldt_ref[...])
    mag = jnp.exp(lre * dt)
    abr = mag * jnp.cos(lim * dt)
    abi = mag * jnp.sin(lim * dt)
    den = lre * lre + lim * lim
    fr = ((abr - 1.0) * lre + abi * lim) / den
    fi = (abi * lre - (abr - 1.0) * lim) / den
    abr_ref[...] = abr
    abi_ref[...] = abi
    bbr_ref[...] = fr * bre_ref[...] - fi * bim_ref[...]
    bbi_ref[...] = fr * bim_ref[...] + fi * bre_ref[...]


def _s5_discretise(p):
    _, groups, state = p['s5_lam_re'].shape
    ch = p['s5_b_re'].shape[-1]
    rows = 2 * groups * ch
    spread = lambda t: jnp.broadcast_to(t[:, :, None, :], (2, groups, ch, state)).reshape(rows, state)
    lre = spread(p['s5_lam_re'])
    lim = spread(p['s5_lam_im'])
    ldt = spread(jnp.broadcast_to(p['s5_log_dt'][:, :, None], (2, groups, state)))
    bre = jnp.swapaxes(p['s5_b_re'], 2, 3).reshape(rows, state)
    bim = jnp.swapaxes(p['s5_b_im'], 2, 3).reshape(rows, state)
    out = jax.ShapeDtypeStruct((rows, state), F32)
    abr, abi, bbr, bbi = pl.pallas_call(_s5_disc_kernel, out_shape=[out] * 4)(lre, lim, ldt, bre, bim)
    eye = jnp.eye(groups, dtype=F32)
    chain = lambda t: jnp.repeat(t.reshape(2, groups, ch, state)[:, :, 0, :].reshape(2, groups * state),
                                 SUBLANES // 2, axis=0)

    def bmat(t):
        t = t.reshape(2, groups, ch, state)
        return (t[:, :, :, None, :] * eye[None, :, None, :, None]).reshape(2, groups * ch, groups * state)

    def cmat(t):
        t = jnp.swapaxes(t, 2, 3)
        return (t[:, :, :, None, :] * eye[None, :, None, :, None]).reshape(2, groups * state, groups * ch)

    b_all = jnp.concatenate([bmat(bbr), bmat(bbi)], axis=-1).astype(MXU_DTYPE)
    c_all = jnp.stack([cmat(p['s5_c_re']), cmat(p['s5_c_im'])], axis=1).astype(MXU_DTYPE)
    return chain(abr), chain(abi), b_all, c_all


def _s5_kernel(tc, u_ref, b_ref, c_ref, ar_ref, ai_ref, h0r_ref, h0i_ref,
               y_ref, hfr_ref, hfi_ref, bur, bui, hr, hi):
    i = pl.program_id(0)
    rows = tc * SUBLANES
    ns = ar_ref.shape[1]

    @pl.when(i == 0)
    def _():
        hr[...] = h0r_ref[...]
        hi[...] = h0i_ref[...]

    u = u_ref[...].reshape(rows, u_ref.shape[-1])
    fwd_row = (lax.broadcasted_iota(jnp.int32, (rows, 1), 0) % SUBLANES) < (SUBLANES // 2)
    bu = jnp.where(fwd_row, _mm(u, b_ref[0]), _mm(u, b_ref[1]))
    bur[...] = bu[:, :ns]
    bui[...] = bu[:, ns:]
    cw = 512
    for cb in range(ns // cw):
        cols = slice(cb * cw, (cb + 1) * cw)
        ar = ar_ref[:, cols]
        ai = ai_ref[:, cols]

        def body(t, carry, cols=cols, ar=ar, ai=ai):
            h_r, h_i = carry
            o = pl.multiple_of(t * SUBLANES, SUBLANES)
            n_r = ar * h_r - ai * h_i + bur[pl.ds(o, SUBLANES), cols]
            n_i = ar * h_i + ai * h_r + bui[pl.ds(o, SUBLANES), cols]
            bur[pl.ds(o, SUBLANES), cols] = n_r
            bui[pl.ds(o, SUBLANES), cols] = n_i
            return n_r, n_i

        h_r, h_i = lax.fori_loop(0, tc, body, (hr[:, cols], hi[:, cols]), unroll=8)
        hr[:, cols] = h_r
        hi[:, cols] = h_i
    h_re = bur[...]
    h_im = bui[...]
    y = jnp.where(fwd_row,
                  _mm(h_re, c_ref[0, 0]) - _mm(h_im, c_ref[0, 1]),
                  _mm(h_re, c_ref[1, 0]) - _mm(h_im, c_ref[1, 1]))
    y_ref[...] = y.reshape(y_ref.shape)
    hfr_ref[...] = hr[...]
    hfi_ref[...] = hi[...]


def _s5_scan(u2, disc, h0):
    ar, ai, b_all, c_all = disc
    n, chains, w = u2.shape
    ns = ar.shape[1]
    tc = min(64, n)
    full = lambda a: pl.BlockSpec(a.shape, lambda i, nd=a.ndim: (0,) * nd)
    st = jax.ShapeDtypeStruct((chains, ns), F32)
    y2, hfr, hfi = pl.pallas_call(
        functools.partial(_s5_kernel, tc),
        grid=(n // tc,),
        in_specs=[pl.BlockSpec((tc, chains, w), lambda i: (i, 0, 0)),
                  full(b_all), full(c_all), full(ar), full(ai), full(h0[0]), full(h0[1])],
        out_specs=[pl.BlockSpec((tc, chains, w), lambda i: (i, 0, 0)), full(ar), full(ar)],
        out_shape=[jax.ShapeDtypeStruct((n, chains, w), F32), st, st],
        scratch_shapes=[pltpu.VMEM((tc * chains, ns), F32), pltpu.VMEM((tc * chains, ns), F32),
                        pltpu.VMEM((chains, ns), F32), pltpu.VMEM((chains, ns), F32)],
        compiler_params=_cparams("arbitrary"),
    )(u2, b_all, c_all, ar, ai, h0[0], h0[1])
    return y2, (hfr, hfi)


def _to_scan_order(u, column_major):
    bsz, n, w = u.shape
    if column_major:
        return u.reshape(bsz, n // GRID_W, GRID_W, w).transpose(2, 1, 0, 3).reshape(n, bsz, w)
    return u.transpose(1, 0, 2)


def _from_scan_order(y, column_major):
    n, bsz, w = y.shape
    if column_major:
        return y.reshape(GRID_W, n // GRID_W, bsz, w).transpose(2, 1, 0, 3).reshape(bsz, n, w)
    return y.transpose(1, 0, 2)


def _mixout_kernel(x_ref, mod_ref, yf_ref, yb_ref, bonf_ref, bonb_ref, gate_ref, of_ref, ob_ref, zg_ref,
                   s5f_ref, s5b_ref, u5_ref, lnxg_ref, lnxb_ref, gdg_ref, s5d_ref, gluw_ref, glub_ref,
                   wout_ref, o_ref):
    hmean = _head_mean_matrix(yf_ref.shape[-1])
    y = yf_ref[...] + yb_ref[...]
    yc = y - _mm_exact_rhs(y, hmean, STAT_TERMS)
    yn = yc * lax.rsqrt(_mm_exact_rhs(yc * yc, hmean, STAT_TERMS) + LNX_EPS)
    y_rw = (yn * lnxg_ref[...] + lnxb_ref[...] + bonf_ref[...] + bonb_ref[...]) * gate_ref[...]
    o = of_ref[...] + ob_ref[...]
    o = o * lax.rsqrt(_mm_exact_rhs(o * o, hmean, STAT_TERMS) + NORM_EPS) * gdg_ref[...]
    y_gd = o * _silu(zg_ref[...])
    y5 = _gelu_tanh(s5f_ref[...] + s5b_ref[...] + s5d_ref[...] * u5_ref[...])
    y5 = y5 * _sigmoid(_mm(y5, gluw_ref[...]) + glub_ref[...])
    mix = jnp.concatenate([y_rw, y_gd, y5], axis=1)
    o_ref[...] = x_ref[...] + mod_ref[2:3, :] * _mm(mix, wout_ref[...])


def _mix_out(xs, mod, mod_row, toks, params):
    bsz, n, d = xs.shape
    tm = min(512, n)
    tok_spec = lambda a: pl.BlockSpec((None, tm, a.shape[-1]), lambda b, i: (b, i, 0))
    full = lambda a: pl.BlockSpec(a.shape, lambda b, i, nd=a.ndim: (0,) * nd)
    return pl.pallas_call(
        _mixout_kernel,
        grid=(bsz, n // tm),
        in_specs=[tok_spec(xs), pl.BlockSpec((None, 6, d), lambda b, i: (mod_row(b), 0, 0))]
        + [tok_spec(a) for a in toks] + [full(a) for a in params],
        out_specs=tok_spec(xs),
        out_shape=jax.ShapeDtypeStruct(xs.shape, F32),
        compiler_params=_cparams("parallel", "parallel"),
    )(xs, mod, *toks, *params)


def _ffn_kernel(hidden, final, x_ref, mod_ref, g_ref, wgu_ref, wd_ref, fg_ref, o_ref):
    x = x_ref[...]
    ms = jnp.mean(x * x, axis=-1, keepdims=True)
    h = x * lax.rsqrt(ms + NORM_EPS) * g_ref[...]
    h = h * (1.0 + mod_ref[4:5, :]) + mod_ref[3:4, :]
    gu = _mm(h, wgu_ref[...])
    act = _silu(gu[:, :hidden]) * gu[:, hidden:]
    o = x + mod_ref[5:6, :] * _mm(act, wd_ref[...])
    if final:
        ms = jnp.mean(o * o, axis=-1, keepdims=True)
        o = o * lax.rsqrt(ms + NORM_EPS) * fg_ref[...]
    o_ref[...] = o


def _ffn(xs, mod, mod_row, gain, w_gu, w_down, final_gain, final):
    bsz, n, d = xs.shape
    hidden = w_down.shape[0]
    tm = min(256, n)
    const = lambda a: pl.BlockSpec(a.shape, lambda b, i, nd=a.ndim: (0,) * nd,
                                   pipeline_mode=pl.Buffered(1))
    tok = pl.BlockSpec((None, tm, d), lambda b, i: (b, i, 0))
    return pl.pallas_call(
        functools.partial(_ffn_kernel, hidden, final),
        grid=(bsz, n // tm),
        in_specs=[tok, pl.BlockSpec((None, 6, d), lambda b, i: (mod_row(b), 0, 0)),
                  const(gain), const(w_gu), const(w_down), const(final_gain)],
        out_specs=tok,
        out_shape=jax.ShapeDtypeStruct(xs.shape, F32),
        compiler_params=_cparams("parallel", "parallel"),
    )(xs, mod, gain, w_gu, w_down, final_gain)


def kernel(x, c, ctx, c_ctx, norm1_g, norm2_g, final_g, ada_w, ada_b, w_in, w_out,
           rw_mu, rw_w0, rw_wup, rw_a0, rw_aup, rw_gup, rw_kk, rw_ka, rw_rk, rw_lnx_g, rw_lnx_b,
           gd_conv, gd_a_log, gd_dt_bias, gd_norm_g,
           s5_lam_re, s5_lam_im, s5_log_dt, s5_b_re, s5_b_im, s5_c_re, s5_c_im, s5_d, s5_glu_w, s5_glu_b,
           ffn_w_gate, ffn_w_up, ffn_w_down):
    bsz, n, d = x.shape
    depth = w_in.shape[0]
    rw_w = rw_kk.shape[-1]
    gd_w = gd_conv.shape[-1] // 3
    s5_w = s5_d.shape[-1]
    rw_heads = rw_w // HEAD_DIM
    gd_heads = gd_w // HEAD_DIM
    in_rw = rw_mu.shape[-1]
    assert 2 * bsz == SUBLANES and n % GRID_W == 0 and n % CHUNK == 0 and ctx.shape[1] % CHUNK == 0

    cond = jnp.concatenate([c, c_ctx[None, :], jnp.zeros((SUBLANES - bsz - 1, d), F32)], axis=0)
    mod_all = _ada_modulation(cond, ada_w, ada_b).reshape(depth, SUBLANES, 6, d)
    x_row = lambda b: b
    ctx_row = lambda b: bsz

    o_gd = in_rw
    o_ba = o_gd + 3 * gd_w
    o_gate = o_ba + 4 * gd_heads
    o_s5 = o_gate + gd_w
    splits = (in_rw, 3 * gd_w, gd_w, s5_w, LANES)
    w_perm = jnp.concatenate(
        [w_in[:, :, :o_gd], w_in[:, :, o_gd:o_ba], w_in[:, :, o_gate:o_s5], w_in[:, :, o_s5:],
         w_in[:, :, o_ba:o_gate], jnp.zeros((depth, d, LANES - 4 * gd_heads), F32)], axis=-1).astype(MXU_DTYPE)
    w_gu = jnp.concatenate([ffn_w_gate, ffn_w_up], axis=-1).astype(MXU_DTYPE)
    w_dn = ffn_w_down.astype(MXU_DTYPE)
    w_o = w_out.astype(MXU_DTYPE)
    glu_w = s5_glu_w.astype(MXU_DTYPE)
    row = lambda t: t.reshape(1, -1)

    for l in range(depth):
        p = {'rw_mu': row(rw_mu[l]), 'rw_w0': rw_w0[l], 'rw_wup': rw_wup[l], 'rw_a0': rw_a0[l],
             'rw_aup': rw_aup[l], 'rw_gup': rw_gup[l], 'rw_kk': row(rw_kk[l]), 'rw_ka': row(rw_ka[l]),
             'rw_rk': row(rw_rk[l]), 'gd_conv': gd_conv[l], 'gd_a_log': gd_a_log[l],
             'gd_dt_bias': gd_dt_bias[l], 's5_lam_re': s5_lam_re[l], 's5_lam_im': s5_lam_im[l],
             's5_log_dt': s5_log_dt[l], 's5_b_re': s5_b_re[l], 's5_b_im': s5_b_im[l],
             's5_c_re': s5_c_re[l], 's5_c_im': s5_c_im[l]}
        mix_params = [row(rw_lnx_g[l]), row(rw_lnx_b[l]), row(jnp.tile(gd_norm_g[l], gd_heads)),
                      row(s5_d[l]), glu_w[l], row(s5_glu_b[l]), w_o[l]]
        disc = _s5_discretise(p)
        mod = mod_all[l]
        states = (jnp.zeros((2, bsz, rw_heads, HEAD_DIM, HEAD_DIM), F32),
                  jnp.zeros((2, bsz, gd_heads, HEAD_DIM, HEAD_DIM), F32),
                  (jnp.zeros((SUBLANES, disc[0].shape[1]), F32),) * 2)
        new_streams = []
        for stream, mod_row, column_major, is_ctx in ((ctx, ctx_row, False, True), (x, x_row, True, False)):
            z_rw, z_q, z_g, z_s5, z_ba = _in_projection(stream, mod, mod_row, row(norm1_g[l]), w_perm[l], splits)
            yf, yb, bonf, bonb, gate, s_rw = _rwkv_scan(z_rw, p, states[0])
            of, ob, s_gd = _gdn_scan(z_q, z_ba, p, states[1])
            ut = _to_scan_order(z_s5, column_major)
            y2, s_s5 = _s5_scan(jnp.concatenate([ut, ut[::-1]], axis=1), disc, states[2])
            s5f = _from_scan_order(y2[:, :bsz], column_major)
            s5b = _from_scan_order(y2[::-1, bsz:], column_major)
            states = (s_rw, s_gd, s_s5)
            if is_ctx and l == depth - 1:
                new_streams.append(stream)
                continue
            x1 = _mix_out(stream, mod, mod_row, [yf, yb, bonf, bonb, gate, of, ob, z_g, s5f, s5b, z_s5],
                          mix_params)
            final = (not is_ctx) and l == depth - 1
            new_streams.append(_ffn(x1, mod, mod_row, row(norm2_g[l]), w_gu[l], w_dn[l], row(final_g), final))
        ctx, x = new_streams
    return x
```

```python
import functools
import math

import jax
import jax.numpy as jnp
from jax import lax
from jax.experimental import pallas as pl
from jax.experimental.pallas import tpu as pltpu

F32 = jnp.float32
MXU_DTYPE = jnp.bfloat16
HIGHEST = lax.Precision.HIGHEST

NORM_EPS = 1e-6
LNX_EPS = 64e-5
HEAD_DIM = 64
GRID_W = 64
CHUNK = 64
CHUNKS_PER_STEP = 4
STAT_TERMS = 2
SUM_TERMS = 3
SUBLANES = 8
LANES = 128
RW_DECAY_LORA = 64
RW_ICLR_LORA = 64
RW_GATE_LORA = 128
S5_CH = 16
S5_STATE = 64
VMEM_LIMIT = 56 * 1024 * 1024


def _cparams(*sem):
    return pltpu.CompilerParams(dimension_semantics=sem, vmem_limit_bytes=VMEM_LIMIT)


def _sigmoid(x):
    return 1.0 / (1.0 + jnp.exp(-x))


def _silu(x):
    return x * _sigmoid(x)


def _softplus(x):
    return jnp.maximum(x, 0.0) + jnp.log(1.0 + jnp.exp(-jnp.abs(x)))


def _gelu_tanh(x):
    return 0.5 * x * (1.0 + jnp.tanh(0.7978845608028654 * (x + 0.044715 * x * x * x)))


def _mm(a, b):
    return jnp.dot(a.astype(MXU_DTYPE), b.astype(MXU_DTYPE), preferred_element_type=F32)


def _mm_nt(a, b):
    return lax.dot_general(a.astype(MXU_DTYPE), b.astype(MXU_DTYPE), (((1,), (1,)), ((), ())),
                           preferred_element_type=F32)


def _mm_tn(a, b):
    return lax.dot_general(a.astype(MXU_DTYPE), b.astype(MXU_DTYPE), (((0,), (0,)), ((), ())),
                           preferred_element_type=F32)


def _mm_hi(a, b):
    return jnp.dot(a, b, precision=HIGHEST, preferred_element_type=F32)


def _split(x, terms):
    parts = []
    for _ in range(terms):
        p = x.astype(MXU_DTYPE)
        parts.append(p)
        x = x - p.astype(F32)
    return parts


def _mm_exact_rhs(a, e, terms):
    e = e.astype(MXU_DTYPE)
    return sum(jnp.dot(p, e, preferred_element_type=F32) for p in _split(a, terms))


def _mm_exact_lhs(e, b, terms):
    e = e.astype(MXU_DTYPE)
    return sum(jnp.dot(e, p, preferred_element_type=F32) for p in _split(b, terms))


def _mm_nt_exact_lhs(e, b, terms):
    e = e.astype(MXU_DTYPE)
    return sum(lax.dot_general(e, p, (((1,), (1,)), ((), ())), preferred_element_type=F32)
               for p in _split(b, terms))


def _head_mean_matrix(width):
    r = lax.broadcasted_iota(jnp.int32, (width, width), 0) // HEAD_DIM
    c = lax.broadcasted_iota(jnp.int32, (width, width), 1) // HEAD_DIM
    return jnp.where(r == c, 1.0 / HEAD_DIM, 0.0).astype(F32)


def _tri_masks(n, reverse):
    t = lax.broadcasted_iota(jnp.int32, (n, n), 0)
    j = lax.broadcasted_iota(jnp.int32, (n, n), 1)
    if reverse:
        return j >= t, j > t, t == j
    return j <= t, j < t, t == j


def _shifted(z, prev_row, next_row):
    n = z.shape[0]
    row = lax.broadcasted_iota(jnp.int32, z.shape, 0)
    zp = jnp.where(row == 0, prev_row, pltpu.roll(z, 1, axis=0))
    zn = jnp.where(row == n - 1, next_row, pltpu.roll(z, n - 1, axis=0))
    return zp, zn


def _block_tri_masks(n, reverse):
    t = lax.broadcasted_iota(jnp.int32, (n, n), 0)
    j = lax.broadcasted_iota(jnp.int32, (n, n), 1)
    order = (j >= t) if reverse else (j <= t)
    return order & (t // CHUNK == j // CHUNK)


def _pair_index(shape):
    t = lax.broadcasted_iota(jnp.int32, shape, 0)
    j = lax.broadcasted_iota(jnp.int32, shape, 1) % HEAD_DIM
    return t, j


def _pair_tri_masks(reverse):
    t, j = _pair_index((CHUNK, 2 * CHUNK))
    if reverse:
        return j >= t, j > t
    return j <= t, j < t


def _bd(x):
    x = x.astype(MXU_DTYPE)
    lo = lax.broadcasted_iota(jnp.int32, x.shape, 1) < HEAD_DIM
    zero = jnp.zeros_like(x)
    return jnp.concatenate([jnp.where(lo, x, zero), jnp.where(lo, zero, x)], axis=0)


def _bd_mask(n):
    r = lax.broadcasted_iota(jnp.int32, (n, n), 0) // HEAD_DIM
    c = lax.broadcasted_iota(jnp.int32, (n, n), 1) // HEAD_DIM
    return r == c


def _unit_triangular_inverses(nmats):
    c = nmats[0].shape[0]
    t, j = _pair_index(nmats[0].shape)
    eye = (t == j).astype(F32)
    s = SUBLANES
    same = t // s == j // s
    ps = [jnp.where(same, m, 0.0) for m in nmats]
    xs = [eye + p for p in ps]
    for _ in range(2):
        ps = [_mm(p, _bd(p)) for p in ps]
        xs = [x + _mm(p, _bd(x)) for p, x in zip(ps, xs)]
    while s < c:
        sel = (t // (2 * s) == j // (2 * s)) & (t // s != j // s)
        cx = [_mm(jnp.where(sel, m, 0.0), _bd(x)) for m, x in zip(nmats, xs)]
        xs = [x + _mm(x, _bd(y)) for x, y in zip(xs, cx)]
        s *= 2
    return xs


def _scan_state_free_stage(items):
    c, n = items[0]['v'].shape
    keep = _bd_mask(n)
    xs = _unit_triangular_inverses([it['a_ab'] for it in items])
    av = [_mm(jnp.concatenate([it['a_ak'], it['a_rk']], axis=0), _bd(it['v'])) for it in items]
    ws = [_mm(x, jnp.concatenate([_bd(it['at']), _bd(a[:c])], axis=1)) for x, it, a in zip(xs, items, av)]
    kv = [jnp.where(keep, _mm_tn(it['v'], it['kg']), 0.0) for it in items]
    for it, w, a, m in zip(items, ws, av, kv):
        it['lhs'] = jnp.concatenate([w[:, :n], it['rt']], axis=0)
        it['p2'] = w[:, n:]
        it['y0'] = a[c:]
        it['kv'] = m


def _scan_state_stage(items, states):
    c, n = items[0]['p2'].shape
    keep = _bd_mask(n)
    ps = [_mm_nt(it['lhs'], s) for it, s in zip(items, states)]
    us = [p[:c] + it['p2'] for p, it in zip(ps, items)]
    ys = [p[c:] + _mm(it['a_rb'], _bd(u)) + it['y0'] for p, it, u in zip(ps, items, us)]
    new = [s * it['dec'] + jnp.where(keep, _mm_tn(u, it['bg']), 0.0) + it['kv']
           for s, it, u in zip(states, items, us)]
    return ys, new


def _run_chains(items, pairs, nb, s_scr):
    _scan_state_free_stage(list(items.values()))
    chains = [(d, p) for d in range(2) for p in range(pairs)]
    states = [s_scr[d, p] for d, p in chains]
    ys = {}
    for step in range(nb):
        keys = [(d, p, nb - 1 - step if d else step) for d, p in chains]
        out, states = _scan_state_stage([items[k] for k in keys], states)
        ys.update(zip(keys, out))
    for (d, p), s in zip(chains, states):
        s_scr[d, p] = s
    return [jnp.concatenate([jnp.concatenate([ys[(d, p, j)] for p in range(pairs)], axis=1)
                             for j in range(nb)], axis=0) for d in range(2)]


def _dplr_items(d, heads, nb, r, lw, k, v, a, b):
    n, w = r.shape
    incl, strict = _pair_tri_masks(d == 1)
    g = _mm_exact_lhs(_block_tri_masks(n, d == 1), lw, SUM_TERMS)
    gts = [jnp.sum(lw[j * CHUNK:(j + 1) * CHUNK], axis=0, keepdims=True) for j in range(nb)]
    gtb = jnp.concatenate([jnp.broadcast_to(gt, (CHUNK, w)) for gt in gts], axis=0)
    e_neg = jnp.exp(-g)
    e_rem = jnp.exp(gtb - g)
    rt = r * jnp.exp(g)
    at = a * jnp.exp(g - lw)
    bt = b * e_neg
    kt = k * e_neg
    bg = b * e_rem
    kg = k * e_rem
    pw = 2 * CHUNK
    items = {}
    for p in range(heads // 2):
        for j in range(nb):
            rows = slice(j * CHUNK, (j + 1) * CHUNK)
            cols = slice(p * LANES, (p + 1) * LANES)
            am = _mm_nt(jnp.concatenate([at[rows, cols], rt[rows, cols]], axis=0),
                        jnp.concatenate([_bd(bt[rows, cols]), _bd(kt[rows, cols])], axis=0))
            items[(d, p, j)] = dict(
                at=at[rows, cols], rt=rt[rows, cols], v=v[rows, cols], bg=bg[rows, cols], kg=kg[rows, cols],
                dec=jnp.exp(gts[j][:, cols]),
                a_ab=jnp.where(strict, am[:CHUNK, :pw], 0.0), a_ak=jnp.where(strict, am[:CHUNK, pw:], 0.0),
                a_rb=jnp.where(incl, am[CHUNK:, :pw], 0.0), a_rk=jnp.where(incl, am[CHUNK:, pw:], 0.0))
    return items


def _gdn_items(d, heads, nb, q, k, v, beta_all, g_all, lg_all, rows_t):
    n, w = q.shape
    assert CHUNK == HEAD_DIM
    incl, strict = _pair_tri_masks(d == 1)

    def dec(col, row, mask):
        return jnp.where(mask, jnp.exp(jnp.minimum(col - row, 0.0)), 0.0)

    src = lax.broadcasted_iota(jnp.int32, (LANES, w), 0)
    head = lax.broadcasted_iota(jnp.int32, (LANES, w), 1) // HEAD_DIM + d * heads
    beta = _mm_exact_rhs(beta_all, src == head, STAT_TERMS)
    gl = _mm_exact_rhs(jnp.concatenate([g_all, lg_all], axis=0), src == head + 2 * heads, SUM_TERMS)
    g, lg = gl[:n], gl[n:]
    gx = g - lg
    gts = [jnp.sum(lg[j * CHUNK:(j + 1) * CHUNK], axis=0, keepdims=True) for j in range(nb)]
    gtb = jnp.concatenate([jnp.broadcast_to(gt, (CHUNK, w)) for gt in gts], axis=0)
    bk = k * beta
    at = k * jnp.exp(gx)
    rt = q * jnp.exp(g)
    bg = -bk * jnp.exp(gtb - gx)
    kg = bk * jnp.exp(gtb - g)
    items = {}
    for p in range(heads // 2):
        cg = (2 + d) * heads + 2 * p
        for j in range(nb):
            rows = slice(j * CHUNK, (j + 1) * CHUNK)
            cols = slice(p * LANES, (p + 1) * LANES)
            g_col, gx_col = g[rows, cols], gx[rows, cols]
            pair_row = lambda o: jnp.concatenate([rows_t[cg:cg + 1, o + j * CHUNK:o + (j + 1) * CHUNK],
                                                  rows_t[cg + 1:cg + 2, o + j * CHUNK:o + (j + 1) * CHUNK]], axis=1)
            g_row = pair_row(0)
            gx_row = g_row - pair_row(n)
            gram = _mm_nt(jnp.concatenate([k[rows, cols], q[rows, cols]], axis=0), _bd(bk[rows, cols]))
            kbk, qbk = gram[:CHUNK], gram[CHUNK:]
            items[(d, p, j)] = dict(
                at=at[rows, cols], rt=rt[rows, cols], v=v[rows, cols], bg=bg[rows, cols], kg=kg[rows, cols],
                dec=jnp.exp(gts[j][:, cols]),
                a_ab=-kbk * dec(gx_col, gx_row, strict), a_ak=kbk * dec(gx_col, g_row, strict),
                a_rb=-qbk * dec(g_col, gx_row, incl), a_rk=qbk * dec(g_col, g_row, incl))
    return items


def _ada_kernel(cond_ref, w_ref, b_ref, o_ref):
    cnd = cond_ref[...]
    o_ref[...] = _mm_hi(_silu(cnd), w_ref[...]) + b_ref[...]


def _ada_modulation(cond, ada_w, ada_b):
    depth, d, n6 = ada_w.shape
    tn = n6 // 4
    return pl.pallas_call(
        _ada_kernel,
        grid=(depth, n6 // tn),
        in_specs=[pl.BlockSpec((SUBLANES, d), lambda l, j: (0, 0)),
                  pl.BlockSpec((None, d, tn), lambda l, j: (l, 0, j)),
                  pl.BlockSpec((None, 1, tn), lambda l, j: (l, 0, j))],
        out_specs=pl.BlockSpec((None, SUBLANES, tn), lambda l, j: (l, 0, j)),
        out_shape=jax.ShapeDtypeStruct((depth, SUBLANES, n6), F32),
        compiler_params=_cparams("parallel", "parallel"),
    )(cond, ada_w, ada_b.reshape(depth, 1, n6))


def _in_kernel(nt, splits, x_ref, xp_ref, xn_ref, mod_ref, g_ref, w_ref, mu_ref, kk_ref, conv_ref,
               rw_ref, kkn_ref, q_ref, g_out_ref, s5_ref, ba_ref):
    i = pl.program_id(1)
    in_rw, gd3, gd_w, s5_w, _ = splits
    rw_w = kk_ref.shape[-1]

    def norm_mod(x):
        ms = jnp.mean(x * x, axis=-1, keepdims=True)
        h = x * lax.rsqrt(ms + NORM_EPS) * g_ref[...]
        return h * (1.0 + mod_ref[1:2, :]) + mod_ref[0:1, :]

    z = _mm(norm_mod(x_ref[...]), w_ref[...])
    ns = in_rw + gd3
    zh = _mm(norm_mod(jnp.concatenate([xp_ref[...], xn_ref[...]], axis=0)), w_ref[:, :ns])
    prev_row = jnp.where(i > 0, zh[SUBLANES - 1:SUBLANES], 0.0)
    next_row = jnp.where(i < nt - 1, zh[SUBLANES:SUBLANES + 1], 0.0)
    zs = z[:, :ns]
    zp, zn = _shifted(zs, prev_row, next_row)

    zr = zs[:, :in_rw]
    zr = zr + (0.5 * (zp[:, :in_rw] + zn[:, :in_rw]) - zr) * mu_ref[...]
    hmean = _head_mean_matrix(rw_w)
    kk = zr[:, rw_w:2 * rw_w] * kk_ref[...]
    kkn_ref[...] = kk * lax.rsqrt(_mm_exact_rhs(kk * kk, hmean, STAT_TERMS) * HEAD_DIM + NORM_EPS)
    rw_ref[:, :3 * rw_w] = zr[:, :3 * rw_w]
    lora = zr[:, 3 * rw_w:3 * rw_w + LANES]
    is_decay = lax.broadcasted_iota(jnp.int32, lora.shape, 1) < RW_DECAY_LORA
    rw_ref[:, 3 * rw_w:3 * rw_w + LANES] = jnp.where(is_decay, jnp.tanh(lora), lora)
    rw_ref[:, 3 * rw_w + LANES:] = zr[:, 3 * rw_w + LANES:]

    sq = slice(in_rw, ns)
    xc = _silu(conv_ref[0:1, :] * zp[:, sq] + conv_ref[1:2, :] * zs[:, sq] + conv_ref[2:3, :] * zn[:, sq])
    q = xc[:, :gd_w]
    k = xc[:, gd_w:2 * gd_w]
    hmean = _head_mean_matrix(gd_w)
    q_ref[:, :gd_w] = q * lax.rsqrt(_mm_exact_rhs(q * q, hmean, STAT_TERMS) * HEAD_DIM + NORM_EPS) * (HEAD_DIM ** -0.5)
    q_ref[:, gd_w:2 * gd_w] = k * lax.rsqrt(_mm_exact_rhs(k * k, hmean, STAT_TERMS) * HEAD_DIM + NORM_EPS)
    q_ref[:, 2 * gd_w:] = xc[:, 2 * gd_w:]

    off = ns
    for o_ref, width in zip((g_out_ref, s5_ref, ba_ref), splits[2:]):
        o_ref[...] = z[:, off:off + width]
        off += width


def _in_projection(xs, mod, mod_row, gain, w_perm, splits, mu, kk, conv):
    bsz, n, d = xs.shape
    tm = min(512, n)
    nt = n // tm
    tiles = tm // SUBLANES
    total = sum(splits)
    widths = (splits[0], kk.shape[-1]) + tuple(splits[1:])
    const = lambda a: pl.BlockSpec(a.shape, lambda b, i, nd=a.ndim: (0,) * nd)
    return pl.pallas_call(
        functools.partial(_in_kernel, nt, splits),
        grid=(bsz, nt),
        in_specs=[pl.BlockSpec((None, tm, d), lambda b, i: (b, i, 0)),
                  pl.BlockSpec((None, SUBLANES, d), lambda b, i: (b, jnp.maximum(i * tiles - 1, 0), 0)),
                  pl.BlockSpec((None, SUBLANES, d), lambda b, i: (b, jnp.minimum((i + 1) * tiles, nt * tiles - 1), 0)),
                  pl.BlockSpec((None, 6, d), lambda b, i: (mod_row(b), 0, 0)),
                  pl.BlockSpec((1, d), lambda b, i: (0, 0)),
                  pl.BlockSpec((d, total), lambda b, i: (0, 0)),
                  const(mu), const(kk), const(conv)],
        out_specs=[pl.BlockSpec((None, tm, w), lambda b, i: (b, i, 0)) for w in widths],
        out_shape=[jax.ShapeDtypeStruct((bsz, n, w), F32) for w in widths],
        compiler_params=_cparams("parallel", "parallel"),
    )(xs, xs, xs, mod, gain, w_perm, mu, kk, conv)


def _block_rows(n):
    nb = CHUNKS_PER_STEP
    while n % (nb * CHUNK):
        nb //= 2
    return nb * CHUNK


def _dir_specs(width, nc, rows):
    return [pl.BlockSpec((None, rows, width), lambda b, i: (b, i, 0)),
            pl.BlockSpec((None, rows, width), lambda b, i: (b, nc - 1 - i, 0))]


def _rw_kernel(nc, heads,
               zf_ref, zb_ref, kkf_ref, kkb_ref,
               w0_ref, wup_ref, a0_ref, aup_ref, ka_ref, rk_ref, s0_ref,
               yf_ref, yb_ref, bonf_ref, bonb_ref, sfin_ref, s_scr):
    i = pl.program_id(1)
    w = heads * HEAD_DIM

    @pl.when(i == 0)
    def _():
        s_scr[...] = s0_ref[...]

    hmean = _head_mean_matrix(w)
    nb = zf_ref.shape[0] // CHUNK
    items = {}
    views = ((zf_ref, kkf_ref, bonf_ref), (zb_ref, kkb_ref, bonb_ref))
    for d, (z_ref, kk_ref, bon_ref) in enumerate(views):
        r = z_ref[:, :w]
        k = z_ref[:, w:2 * w]
        v = z_ref[:, 2 * w:3 * w]
        o = 3 * w
        wd_t = z_ref[:, o:o + RW_DECAY_LORA]
        ad = z_ref[:, o + RW_DECAY_LORA:o + RW_DECAY_LORA + RW_ICLR_LORA]
        kk = kk_ref[...]
        lw = -math.exp(-0.5) * _sigmoid(w0_ref[d:d + 1, :] + _mm(wd_t, wup_ref[d]))
        a = _sigmoid(a0_ref[d:d + 1, :] + _mm(ad, aup_ref[d]))
        kd = k * (1.0 + (a - 1.0) * ka_ref[...])
        bon_ref[...] = _mm_exact_rhs(r * kd * rk_ref[...], hmean, STAT_TERMS) * HEAD_DIM * v
        items.update(_dplr_items(d, heads, nb, r, lw, kd, v, -kk, kk * a))
    yf_ref[...], yb_ref[...] = _run_chains(items, heads // 2, nb, s_scr)

    @pl.when(i == nc - 1)
    def _():
        sfin_ref[...] = s_scr[...]


def _rwkv_scan(z_rw, kk, p, s0):
    bsz, n, zw = z_rw.shape
    rows = _block_rows(n)
    nc = n // rows
    w = kk.shape[-1]
    heads = w // HEAD_DIM
    full = lambda a: pl.BlockSpec(a.shape, lambda b, i, nd=a.ndim: (0,) * nd)
    params = [p['rw_w0'], p['rw_wup'], p['rw_a0'], p['rw_aup'], p['rw_ka'], p['rw_rk']]
    state_spec = pl.BlockSpec((2, None, heads // 2, LANES, LANES), lambda b, i: (0, b, 0, 0, 0))
    fwd, bwd = _dir_specs(w, nc, rows)
    tok = jax.ShapeDtypeStruct((bsz, n, w), F32)
    return pl.pallas_call(
        functools.partial(_rw_kernel, nc, heads),
        grid=(bsz, nc),
        in_specs=_dir_specs(zw, nc, rows) + [fwd, bwd] + [full(a) for a in params] + [state_spec],
        out_specs=[fwd, bwd, fwd, bwd, state_spec],
        out_shape=[tok, tok, tok, tok, jax.ShapeDtypeStruct(s0.shape, F32)],
        scratch_shapes=[pltpu.VMEM((2, heads // 2, LANES, LANES), F32)],
        compiler_params=_cparams("parallel", "arbitrary"),
    )(z_rw, z_rw, kk, kk, *params, s0)


def _gd_kernel(nc, heads,
               zf_ref, zb_ref, baf_ref, bab_ref, alog_ref, dtb_ref, s0_ref,
               of_ref, ob_ref, sfin_ref, s_scr):
    i = pl.program_id(1)
    w = heads * HEAD_DIM

    @pl.when(i == 0)
    def _():
        s_scr[...] = s0_ref[...]

    lane_eye = _tri_masks(LANES, False)[2].astype(F32)
    nb = zf_ref.shape[0] // CHUNK
    items = {}
    for d, (z_ref, ba_ref) in enumerate(((zf_ref, baf_ref), (zb_ref, bab_ref))):
        q = z_ref[:, :w]
        k = z_ref[:, w:2 * w]
        v = z_ref[:, 2 * w:]
        ba = ba_ref[...]
        beta_all = _sigmoid(ba)
        lg_all = -jnp.exp(alog_ref[...]) * _softplus(ba + dtb_ref[...])
        g_all = _mm_exact_lhs(_block_tri_masks(ba.shape[0], d == 1), lg_all, SUM_TERMS)
        rows_t = _mm_nt_exact_lhs(lane_eye, jnp.concatenate([g_all, lg_all], axis=0), SUM_TERMS)
        items.update(_gdn_items(d, heads, nb, q, k, v, beta_all, g_all, lg_all, rows_t))
    of_ref[...], ob_ref[...] = _run_chains(items, heads // 2, nb, s_scr)

    @pl.when(i == nc - 1)
    def _():
        sfin_ref[...] = s_scr[...]


def _gdn_scan(z_q, z_ba, p, s0):
    bsz, n, zw = z_q.shape
    rows = _block_rows(n)
    nc = n // rows
    w = zw // 3
    heads = w // HEAD_DIM
    full = lambda a: pl.BlockSpec(a.shape, lambda b, i, nd=a.ndim: (0,) * nd)
    pad = lambda t: jnp.pad(t.reshape(1, 2 * heads), ((0, 0), (2 * heads, LANES - 4 * heads)))
    params = [pad(p['gd_a_log']), pad(p['gd_dt_bias'])]
    state_spec = pl.BlockSpec((2, None, heads // 2, LANES, LANES), lambda b, i: (0, b, 0, 0, 0))
    tok = jax.ShapeDtypeStruct((bsz, n, w), F32)
    return pl.pallas_call(
        functools.partial(_gd_kernel, nc, heads),
        grid=(bsz, nc),
        in_specs=_dir_specs(zw, nc, rows) + _dir_specs(LANES, nc, rows) + [full(a) for a in params]
        + [state_spec],
        out_specs=_dir_specs(w, nc, rows) + [state_spec],
        out_shape=[tok, tok, jax.ShapeDtypeStruct(s0.shape, F32)],
        scratch_shapes=[pltpu.VMEM((2, heads // 2, LANES, LANES), F32)],
        compiler_params=_cparams("parallel", "arbitrary"),
    )(z_q, z_q, z_ba, z_ba, *params, s0)


def _s5_disc_kernel(lre_ref, lim_ref, ldt_ref, bre_ref, bim_ref, abr_ref, abi_ref, bbr_ref, bbi_ref):
    lre = lre_ref[...]
    lim = lim_ref[...]
    dt = jnp.exp(ldt_ref[...])
    mag = jnp.exp(lre * dt)
    abr = mag * jnp.cos(lim * dt)
    abi = mag * jnp.sin(lim * dt)
    den = lre * lre + lim * lim
    fr = ((abr - 1.0) * lre + abi * lim) / den
    fi = (abi * lre - (abr - 1.0) * lim) / den
    abr_ref[...] = abr
    abi_ref[...] = abi
    bbr_ref[...] = fr * bre_ref[...] - fi * bim_ref[...]
    bbi_ref[...] = fr * bim_ref[...] + fi * bre_ref[...]


def _s5_discretise(p):
    _, groups, state = p['s5_lam_re'].shape
    ch = p['s5_b_re'].shape[-1]
    rows = 2 * groups * ch
    spread = lambda t: jnp.broadcast_to(t[:, :, None, :], (2, groups, ch, state)).reshape(rows, state)
    lre = spread(p['s5_lam_re'])
    lim = spread(p['s5_lam_im'])
    ldt = spread(jnp.broadcast_to(p['s5_log_dt'][:, :, None], (2, groups, state)))
    bre = jnp.swapaxes(p['s5_b_re'], 2, 3).reshape(rows, state)
    bim = jnp.swapaxes(p['s5_b_im'], 2, 3).reshape(rows, state)
    out = jax.ShapeDtypeStruct((rows, state), F32)
    abr, abi, bbr, bbi = pl.pallas_call(_s5_disc_kernel, out_shape=[out] * 4)(lre, lim, ldt, bre, bim)
    eye = jnp.eye(groups, dtype=F32)
    chain = lambda t: jnp.repeat(t.reshape(2, groups, ch, state)[:, :, 0, :].reshape(2, groups * state),
                                 SUBLANES // 2, axis=0)

    def bmat(t):
        t = t.reshape(2, groups, ch, state)
        return (t[:, :, :, None, :] * eye[None, :, None, :, None]).reshape(2, groups * ch, groups * state)

    def cmat(t):
        t = jnp.swapaxes(t, 2, 3)
        return (t[:, :, :, None, :] * eye[None, :, None, :, None]).reshape(2, groups * state, groups * ch)

    b_all = jnp.concatenate([bmat(bbr), bmat(bbi)], axis=-1).astype(MXU_DTYPE)
    c_all = jnp.stack([cmat(p['s5_c_re']), cmat(p['s5_c_im'])], axis=1).astype(MXU_DTYPE)
    return chain(abr), chain(abi), b_all, c_all


def _s5_kernel(tc, u_ref, b_ref, c_ref, ar_ref, ai_ref, h0r_ref, h0i_ref,
               y_ref, hfr_ref, hfi_ref, bur, bui, hr, hi):
    i = pl.program_id(0)
    rows = tc * SUBLANES
    ns = ar_ref.shape[1]

    @pl.when(i == 0)
    def _():
        hr[...] = h0r_ref[...]
        hi[...] = h0i_ref[...]

    u = u_ref[...].reshape(rows, u_ref.shape[-1])
    fwd_row = (lax.broadcasted_iota(jnp.int32, (rows, 1), 0) % SUBLANES) < (SUBLANES // 2)
    bu = jnp.where(fwd_row, _mm(u, b_ref[0]), _mm(u, b_ref[1]))
    bur[...] = bu[:, :ns]
    bui[...] = bu[:, ns:]
    cw = 512
    for cb in range(ns // cw):
        cols = slice(cb * cw, (cb + 1) * cw)
        ar = ar_ref[:, cols]
        ai = ai_ref[:, cols]

        def body(t, carry, cols=cols, ar=ar, ai=ai):
            h_r, h_i = carry
            o = pl.multiple_of(t * SUBLANES, SUBLANES)
            n_r = ar * h_r - ai * h_i + bur[pl.ds(o, SUBLANES), cols]
            n_i = ar * h_i + ai * h_r + bui[pl.ds(o, SUBLANES), cols]
            bur[pl.ds(o, SUBLANES), cols] = n_r
            bui[pl.ds(o, SUBLANES), cols] = n_i
            return n_r, n_i

        h_r, h_i = lax.fori_loop(0, tc, body, (hr[:, cols], hi[:, cols]), unroll=8)
        hr[:, cols] = h_r
        hi[:, cols] = h_i
    h_re = bur[...]
    h_im = bui[...]
    y = jnp.where(fwd_row,
                  _mm(h_re, c_ref[0, 0]) - _mm(h_im, c_ref[0, 1]),
                  _mm(h_re, c_ref[1, 0]) - _mm(h_im, c_ref[1, 1]))
    y_ref[...] = y.reshape(y_ref.shape)
    hfr_ref[...] = hr[...]
    hfi_ref[...] = hi[...]


def _s5_scan(u2, disc, h0):
    ar, ai, b_all, c_all = disc
    n, chains, w = u2.shape
    ns = ar.shape[1]
    tc = min(64, n)
    full = lambda a: pl.BlockSpec(a.shape, lambda i, nd=a.ndim: (0,) * nd)
    st = jax.ShapeDtypeStruct((chains, ns), F32)
    y2, hfr, hfi = pl.pallas_call(
        functools.partial(_s5_kernel, tc),
        grid=(n // tc,),
        in_specs=[pl.BlockSpec((tc, chains, w), lambda i: (i, 0, 0)),
                  full(b_all), full(c_all), full(ar), full(ai), full(h0[0]), full(h0[1])],
        out_specs=[pl.BlockSpec((tc, chains, w), lambda i: (i, 0, 0)), full(ar), full(ar)],
        out_shape=[jax.ShapeDtypeStruct((n, chains, w), F32), st, st],
        scratch_shapes=[pltpu.VMEM((tc * chains, ns), F32), pltpu.VMEM((tc * chains, ns), F32),
                        pltpu.VMEM((chains, ns), F32), pltpu.VMEM((chains, ns), F32)],
        compiler_params=_cparams("arbitrary"),
    )(u2, b_all, c_all, ar, ai, h0[0], h0[1])
    return y2, (hfr, hfi)


def _to_scan_order(u, column_major):
    bsz, n, w = u.shape
    if column_major:
        return u.reshape(bsz, n // GRID_W, GRID_W, w).transpose(2, 1, 0, 3).reshape(n, bsz, w)
    return u.transpose(1, 0, 2)


def _from_scan_order(y, column_major):
    n, bsz, w = y.shape
    if column_major:
        return y.reshape(GRID_W, n // GRID_W, bsz, w).transpose(2, 1, 0, 3).reshape(bsz, n, w)
    return y.transpose(1, 0, 2)


def _mixout_kernel(x_ref, mod_ref, gd_ref, yf_ref, yb_ref, bonf_ref, bonb_ref, of_ref, ob_ref, zg_ref,
                   s5f_ref, s5b_ref, u5_ref, gup_ref, lnxg_ref, lnxb_ref, gdg_ref, s5d_ref, gluw_ref, glub_ref,
                   wout_ref, o_ref):
    hmean = _head_mean_matrix(yf_ref.shape[-1])
    y = yf_ref[...] + yb_ref[...]
    yc = y - _mm_exact_rhs(y, hmean, STAT_TERMS)
    yn = yc * lax.rsqrt(_mm_exact_rhs(yc * yc, hmean, STAT_TERMS) + LNX_EPS)
    gate = _mm(_sigmoid(gd_ref[...]), gup_ref[...])
    y_rw = (yn * lnxg_ref[...] + lnxb_ref[...] + bonf_ref[...] + bonb_ref[...]) * gate
    o = of_ref[...] + ob_ref[...]
    o = o * lax.rsqrt(_mm_exact_rhs(o * o, hmean, STAT_TERMS) + NORM_EPS) * gdg_ref[...]
    y_gd = o * _silu(zg_ref[...])
    y5 = _gelu_tanh(s5f_ref[...] + s5b_ref[...] + s5d_ref[...] * u5_ref[...])
    y5 = y5 * _sigmoid(_mm(y5, gluw_ref[...]) + glub_ref[...])
    mix = jnp.concatenate([y_rw, y_gd, y5], axis=1)
    o_ref[...] = x_ref[...] + mod_ref[2:3, :] * _mm(mix, wout_ref[...])


def _mix_out(xs, mod, mod_row, z_rw, toks, params):
    bsz, n, d = xs.shape
    tm = min(512, n)
    tok_spec = lambda a: pl.BlockSpec((None, tm, a.shape[-1]), lambda b, i: (b, i, 0))
    full = lambda a: pl.BlockSpec(a.shape, lambda b, i, nd=a.ndim: (0,) * nd)
    gate_in = z_rw.shape[-1] // RW_GATE_LORA - 1
    assert RW_GATE_LORA == LANES and z_rw.shape[-1] % LANES == 0
    return pl.pallas_call(
        _mixout_kernel,
        grid=(bsz, n // tm),
        in_specs=[tok_spec(xs), pl.BlockSpec((None, 6, d), lambda b, i: (mod_row(b), 0, 0)),
                  pl.BlockSpec((None, tm, LANES), lambda b, i: (b, i, gate_in))]
        + [tok_spec(a) for a in toks] + [full(a) for a in params],
        out_specs=tok_spec(xs),
        out_shape=jax.ShapeDtypeStruct(xs.shape, F32),
        compiler_params=_cparams("parallel", "parallel"),
    )(xs, mod, z_rw, *toks, *params)


def _ffn_kernel(hidden, final, x_ref, mod_ref, g_ref, wgu_ref, wd_ref, fg_ref, o_ref):
    x = x_ref[...]
    ms = jnp.mean(x * x, axis=-1, keepdims=True)
    h = x * lax.rsqrt(ms + NORM_EPS) * g_ref[...]
    h = h * (1.0 + mod_ref[4:5, :]) + mod_ref[3:4, :]
    gu = _mm(h, wgu_ref[...])
    act = _silu(gu[:, :hidden]) * gu[:, hidden:]
    o = x + mod_ref[5:6, :] * _mm(act, wd_ref[...])
    if final:
        ms = jnp.mean(o * o, axis=-1, keepdims=True)
        o = o * lax.rsqrt(ms + NORM_EPS) * fg_ref[...]
    o_ref[...] = o


def _ffn(xs, mod, mod_row, gain, w_gu, w_down, final_gain, final):
    bsz, n, d = xs.shape
    hidden = w_down.shape[0]
    tm = min(256, n)
    const = lambda a: pl.BlockSpec(a.shape, lambda b, i, nd=a.ndim: (0,) * nd,
                                   pipeline_mode=pl.Buffered(1))
    tok = pl.BlockSpec((None, tm, d), lambda b, i: (b, i, 0))
    return pl.pallas_call(
        functools.partial(_ffn_kernel, hidden, final),
        grid=(bsz, n // tm),
        in_specs=[tok, pl.BlockSpec((None, 6, d), lambda b, i: (mod_row(b), 0, 0)),
                  const(gain), const(w_gu), const(w_down), const(final_gain)],
        out_specs=tok,
        out_shape=jax.ShapeDtypeStruct(xs.shape, F32),
        compiler_params=_cparams("parallel", "parallel"),
    )(xs, mod, gain, w_gu, w_down, final_gain)


def kernel(x, c, ctx, c_ctx, norm1_g, norm2_g, final_g, ada_w, ada_b, w_in, w_out,
           rw_mu, rw_w0, rw_wup, rw_a0, rw_aup, rw_gup, rw_kk, rw_ka, rw_rk, rw_lnx_g, rw_lnx_b,
           gd_conv, gd_a_log, gd_dt_bias, gd_norm_g,
           s5_lam_re, s5_lam_im, s5_log_dt, s5_b_re, s5_b_im, s5_c_re, s5_c_im, s5_d, s5_glu_w, s5_glu_b,
           ffn_w_gate, ffn_w_up, ffn_w_down):
    bsz, n, d = x.shape
    depth = w_in.shape[0]
    rw_w = rw_kk.shape[-1]
    gd_w = gd_conv.shape[-1] // 3
    s5_w = s5_d.shape[-1]
    rw_heads = rw_w // HEAD_DIM
    gd_heads = gd_w // HEAD_DIM
    in_rw = rw_mu.shape[-1]
    assert 2 * bsz == SUBLANES and n % GRID_W == 0 and n % CHUNK == 0 and ctx.shape[1] % CHUNK == 0
    assert rw_heads % 2 == 0 and gd_heads % 2 == 0 and 2 * HEAD_DIM == LANES

    cond = jnp.concatenate([c, c_ctx[None, :], jnp.zeros((SUBLANES - bsz - 1, d), F32)], axis=0)
    mod_all = _ada_modulation(cond, ada_w, ada_b).reshape(depth, SUBLANES, 6, d)
    x_row = lambda b: b
    ctx_row = lambda b: bsz

    o_gd = in_rw
    o_ba = o_gd + 3 * gd_w
    o_gate = o_ba + 4 * gd_heads
    o_s5 = o_gate + gd_w
    splits = (in_rw, 3 * gd_w, gd_w, s5_w, LANES)
    w_perm = jnp.concatenate(
        [w_in[:, :, :o_gd], w_in[:, :, o_gd:o_ba], w_in[:, :, o_gate:o_s5], w_in[:, :, o_s5:],
         w_in[:, :, o_ba:o_gate], jnp.zeros((depth, d, LANES - 4 * gd_heads), F32)], axis=-1).astype(MXU_DTYPE)
    w_gu = jnp.concatenate([ffn_w_gate, ffn_w_up], axis=-1).astype(MXU_DTYPE)
    w_dn = ffn_w_down.astype(MXU_DTYPE)
    w_o = w_out.astype(MXU_DTYPE)
    glu_w = s5_glu_w.astype(MXU_DTYPE)
    row = lambda t: t.reshape(1, -1)

    for l in range(depth):
        p = {'rw_mu': row(rw_mu[l]), 'rw_w0': rw_w0[l], 'rw_wup': rw_wup[l], 'rw_a0': rw_a0[l],
             'rw_aup': rw_aup[l], 'rw_gup': rw_gup[l], 'rw_kk': row(rw_kk[l]), 'rw_ka': row(rw_ka[l]),
             'rw_rk': row(rw_rk[l]), 'gd_conv': gd_conv[l], 'gd_a_log': gd_a_log[l],
             'gd_dt_bias': gd_dt_bias[l], 's5_lam_re': s5_lam_re[l], 's5_lam_im': s5_lam_im[l],
             's5_log_dt': s5_log_dt[l], 's5_b_re': s5_b_re[l], 's5_b_im': s5_b_im[l],
             's5_c_re': s5_c_re[l], 's5_c_im': s5_c_im[l]}
        mix_params = [rw_gup[l], row(rw_lnx_g[l]), row(rw_lnx_b[l]), row(jnp.tile(gd_norm_g[l], gd_heads)),
                      row(s5_d[l]), glu_w[l], row(s5_glu_b[l]), w_o[l]]
        disc = _s5_discretise(p)
        mod = mod_all[l]
        states = (jnp.zeros((2, bsz, rw_heads // 2, LANES, LANES), F32),
                  jnp.zeros((2, bsz, gd_heads // 2, LANES, LANES), F32),
                  (jnp.zeros((SUBLANES, disc[0].shape[1]), F32),) * 2)
        new_streams = []
        for stream, mod_row, column_major, is_ctx in ((ctx, ctx_row, False, True), (x, x_row, True, False)):
            z_rw, kk, z_q, z_g, z_s5, z_ba = _in_projection(stream, mod, mod_row, row(norm1_g[l]), w_perm[l], splits,
                                                            p['rw_mu'], p['rw_kk'], p['gd_conv'])
            yf, yb, bonf, bonb, s_rw = _rwkv_scan(z_rw, kk, p, states[0])
            of, ob, s_gd = _gdn_scan(z_q, z_ba, p, states[1])
            ut = _to_scan_order(z_s5, column_major)
            y2, s_s5 = _s5_scan(jnp.concatenate([ut, ut[::-1]], axis=1), disc, states[2])
            s5f = _from_scan_order(y2[:, :bsz], column_major)
            s5b = _from_scan_order(y2[::-1, bsz:], column_major)
            states = (s_rw, s_gd, s_s5)
            if is_ctx and l == depth - 1:
                new_streams.append(stream)
                continue
            x1 = _mix_out(stream, mod, mod_row, z_rw, [yf, yb, bonf, bonb, of, ob, z_g, s5f, s5b, z_s5],
                          mix_params)
            final = (not is_ctx) and l == depth - 1
            new_streams.append(_ffn(x1, mod, mod_row, row(norm2_g[l]), w_gu[l], w_dn[l], row(final_g), final))
        ctx, x = new_streams
    return x
```

```python
import functools
import math

import jax
import jax.numpy as jnp
from jax import lax
from jax.experimental import pallas as pl
from jax.experimental.pallas import tpu as pltpu

F32 = jnp.float32
MXU_DTYPE = jnp.bfloat16
HIGHEST = lax.Precision.HIGHEST

NORM_EPS = 1e-6
LNX_EPS = 64e-5
HEAD_DIM = 64
GRID_W = 64
CHUNK = 64
CHUNKS_PER_STEP = 4
STAT_TERMS = 2
SUM_TERMS = 3
SUBLANES = 8
LANES = 128
RW_DECAY_LORA = 64
RW_ICLR_LORA = 64
RW_GATE_LORA = 128
S5_BLOCK = 128
S5_SCAN_LANES = 512
VMEM_LIMIT = 56 * 1024 * 1024


def _cparams(*sem):
    return pltpu.CompilerParams(dimension_semantics=sem, vmem_limit_bytes=VMEM_LIMIT)


def _sigmoid(x):
    return 1.0 / (1.0 + jnp.exp(-x))


def _silu(x):
    return x * _sigmoid(x)


def _softplus(x):
    return jnp.maximum(x, 0.0) + jnp.log(1.0 + jnp.exp(-jnp.abs(x)))


def _gelu_tanh(x):
    return 0.5 * x * (1.0 + jnp.tanh(0.7978845608028654 * (x + 0.044715 * x * x * x)))


def _mm(a, b):
    return jnp.dot(a.astype(MXU_DTYPE), b.astype(MXU_DTYPE), preferred_element_type=F32)


def _mm_nt(a, b):
    return lax.dot_general(a.astype(MXU_DTYPE), b.astype(MXU_DTYPE), (((1,), (1,)), ((), ())),
                           preferred_element_type=F32)


def _mm_tn(a, b):
    return lax.dot_general(a.astype(MXU_DTYPE), b.astype(MXU_DTYPE), (((0,), (0,)), ((), ())),
                           preferred_element_type=F32)


def _mm_hi(a, b):
    return jnp.dot(a, b, precision=HIGHEST, preferred_element_type=F32)


def _split(x, terms):
    parts = []
    for _ in range(terms):
        p = x.astype(MXU_DTYPE)
        parts.append(p)
        x = x - p.astype(F32)
    return parts


def _mm_exact_rhs(a, e, terms):
    e = e.astype(MXU_DTYPE)
    return sum(jnp.dot(p, e, preferred_element_type=F32) for p in _split(a, terms))


def _mm_exact_lhs(e, b, terms):
    e = e.astype(MXU_DTYPE)
    return sum(jnp.dot(e, p, preferred_element_type=F32) for p in _split(b, terms))


def _mm_nt_exact_lhs(e, b, terms):
    e = e.astype(MXU_DTYPE)
    return sum(lax.dot_general(e, p, (((1,), (1,)), ((), ())), preferred_element_type=F32)
               for p in _split(b, terms))


def _head_mean_matrix(width):
    r = lax.broadcasted_iota(jnp.int32, (width, width), 0) // HEAD_DIM
    c = lax.broadcasted_iota(jnp.int32, (width, width), 1) // HEAD_DIM
    return jnp.where(r == c, 1.0 / HEAD_DIM, 0.0).astype(F32)


def _tri_masks(n, reverse):
    t = lax.broadcasted_iota(jnp.int32, (n, n), 0)
    j = lax.broadcasted_iota(jnp.int32, (n, n), 1)
    if reverse:
        return j >= t, j > t, t == j
    return j <= t, j < t, t == j


def _shifted(z, prev_row, next_row):
    n = z.shape[0]
    row = lax.broadcasted_iota(jnp.int32, z.shape, 0)
    zp = jnp.where(row == 0, prev_row, pltpu.roll(z, 1, axis=0))
    zn = jnp.where(row == n - 1, next_row, pltpu.roll(z, n - 1, axis=0))
    return zp, zn


def _block_tri_masks(n, reverse):
    t = lax.broadcasted_iota(jnp.int32, (n, n), 0)
    j = lax.broadcasted_iota(jnp.int32, (n, n), 1)
    order = (j >= t) if reverse else (j <= t)
    return order & (t // CHUNK == j // CHUNK)


def _pair_index(shape):
    t = lax.broadcasted_iota(jnp.int32, shape, 0)
    j = lax.broadcasted_iota(jnp.int32, shape, 1) % HEAD_DIM
    return t, j


def _pair_tri_masks(reverse):
    t, j = _pair_index((CHUNK, 2 * CHUNK))
    if reverse:
        return j >= t, j > t
    return j <= t, j < t


def _bd(x):
    x = x.astype(MXU_DTYPE)
    lo = lax.broadcasted_iota(jnp.int32, x.shape, 1) < HEAD_DIM
    zero = jnp.zeros_like(x)
    return jnp.concatenate([jnp.where(lo, x, zero), jnp.where(lo, zero, x)], axis=0)


def _bd_mask(n):
    r = lax.broadcasted_iota(jnp.int32, (n, n), 0) // HEAD_DIM
    c = lax.broadcasted_iota(jnp.int32, (n, n), 1) // HEAD_DIM
    return r == c


def _unit_triangular_inverses(nmats):
    c = nmats[0].shape[0]
    t, j = _pair_index(nmats[0].shape)
    eye = (t == j).astype(F32)
    s = SUBLANES
    same = t // s == j // s
    ps = [jnp.where(same, m, 0.0) for m in nmats]
    xs = [eye + p for p in ps]
    for _ in range(2):
        ps = [_mm(p, _bd(p)) for p in ps]
        xs = [x + _mm(p, _bd(x)) for p, x in zip(ps, xs)]
    while s < c:
        sel = (t // (2 * s) == j // (2 * s)) & (t // s != j // s)
        cx = [_mm(jnp.where(sel, m, 0.0), _bd(x)) for m, x in zip(nmats, xs)]
        xs = [x + _mm(x, _bd(y)) for x, y in zip(xs, cx)]
        s *= 2
    return xs


def _scan_state_free_stage(items):
    c, n = items[0]['v'].shape
    keep = _bd_mask(n)
    xs = _unit_triangular_inverses([it['a_ab'] for it in items])
    av = [_mm(jnp.concatenate([it['a_ak'], it['a_rk']], axis=0), _bd(it['v'])) for it in items]
    ws = [_mm(x, jnp.concatenate([_bd(it['at']), _bd(a[:c])], axis=1)) for x, it, a in zip(xs, items, av)]
    kv = [jnp.where(keep, _mm_tn(it['v'], it['kg']), 0.0) for it in items]
    for it, w, a, m in zip(items, ws, av, kv):
        it['lhs'] = jnp.concatenate([w[:, :n], it['rt']], axis=0)
        it['p2'] = w[:, n:]
        it['y0'] = a[c:]
        it['kv'] = m


def _scan_state_stage(items, states):
    c, n = items[0]['p2'].shape
    keep = _bd_mask(n)
    ps = [_mm_nt(it['lhs'], s) for it, s in zip(items, states)]
    us = [p[:c] + it['p2'] for p, it in zip(ps, items)]
    ys = [p[c:] + _mm(it['a_rb'], _bd(u)) + it['y0'] for p, it, u in zip(ps, items, us)]
    new = [s * it['dec'] + jnp.where(keep, _mm_tn(u, it['bg']), 0.0) + it['kv']
           for s, it, u in zip(states, items, us)]
    return ys, new


def _run_chains(items, pairs, nb, s_scr):
    _scan_state_free_stage(list(items.values()))
    chains = [(d, p) for d in range(2) for p in range(pairs)]
    states = [s_scr[d, p] for d, p in chains]
    ys = {}
    for step in range(nb):
        keys = [(d, p, nb - 1 - step if d else step) for d, p in chains]
        out, states = _scan_state_stage([items[k] for k in keys], states)
        ys.update(zip(keys, out))
    for (d, p), s in zip(chains, states):
        s_scr[d, p] = s
    return [jnp.concatenate([jnp.concatenate([ys[(d, p, j)] for p in range(pairs)], axis=1)
                             for j in range(nb)], axis=0) for d in range(2)]


def _dplr_items(d, heads, nb, r, lw, k, v, a, b):
    n, w = r.shape
    incl, strict = _pair_tri_masks(d == 1)
    g = _mm_exact_lhs(_block_tri_masks(n, d == 1), lw, SUM_TERMS)
    gts = [jnp.sum(lw[j * CHUNK:(j + 1) * CHUNK], axis=0, keepdims=True) for j in range(nb)]
    gtb = jnp.concatenate([jnp.broadcast_to(gt, (CHUNK, w)) for gt in gts], axis=0)
    e_neg = jnp.exp(-g)
    e_rem = jnp.exp(gtb - g)
    rt = r * jnp.exp(g)
    at = a * jnp.exp(g - lw)
    bt = b * e_neg
    kt = k * e_neg
    bg = b * e_rem
    kg = k * e_rem
    pw = 2 * CHUNK
    items = {}
    for p in range(heads // 2):
        for j in range(nb):
            rows = slice(j * CHUNK, (j + 1) * CHUNK)
            cols = slice(p * LANES, (p + 1) * LANES)
            am = _mm_nt(jnp.concatenate([at[rows, cols], rt[rows, cols]], axis=0),
                        jnp.concatenate([_bd(bt[rows, cols]), _bd(kt[rows, cols])], axis=0))
            items[(d, p, j)] = dict(
                at=at[rows, cols], rt=rt[rows, cols], v=v[rows, cols], bg=bg[rows, cols], kg=kg[rows, cols],
                dec=jnp.exp(gts[j][:, cols]),
                a_ab=jnp.where(strict, am[:CHUNK, :pw], 0.0), a_ak=jnp.where(strict, am[:CHUNK, pw:], 0.0),
                a_rb=jnp.where(incl, am[CHUNK:, :pw], 0.0), a_rk=jnp.where(incl, am[CHUNK:, pw:], 0.0))
    return items


def _gdn_items(d, heads, nb, q, k, v, beta_all, g_all, lg_all, rows_t):
    n, w = q.shape
    assert CHUNK == HEAD_DIM
    incl, strict = _pair_tri_masks(d == 1)

    def dec(col, row, mask):
        return jnp.where(mask, jnp.exp(jnp.minimum(col - row, 0.0)), 0.0)

    src = lax.broadcasted_iota(jnp.int32, (LANES, w), 0)
    head = lax.broadcasted_iota(jnp.int32, (LANES, w), 1) // HEAD_DIM + d * heads
    beta = _mm_exact_rhs(beta_all, src == head, STAT_TERMS)
    gl = _mm_exact_rhs(jnp.concatenate([g_all, lg_all], axis=0), src == head + 2 * heads, SUM_TERMS)
    g, lg = gl[:n], gl[n:]
    gx = g - lg
    gts = [jnp.sum(lg[j * CHUNK:(j + 1) * CHUNK], axis=0, keepdims=True) for j in range(nb)]
    gtb = jnp.concatenate([jnp.broadcast_to(gt, (CHUNK, w)) for gt in gts], axis=0)
    bk = k * beta
    at = k * jnp.exp(gx)
    rt = q * jnp.exp(g)
    bg = -bk * jnp.exp(gtb - gx)
    kg = bk * jnp.exp(gtb - g)
    items = {}
    for p in range(heads // 2):
        cg = (2 + d) * heads + 2 * p
        for j in range(nb):
            rows = slice(j * CHUNK, (j + 1) * CHUNK)
            cols = slice(p * LANES, (p + 1) * LANES)
            g_col, gx_col = g[rows, cols], gx[rows, cols]
            pair_row = lambda o: jnp.concatenate([rows_t[cg:cg + 1, o + j * CHUNK:o + (j + 1) * CHUNK],
                                                  rows_t[cg + 1:cg + 2, o + j * CHUNK:o + (j + 1) * CHUNK]], axis=1)
            g_row = pair_row(0)
            gx_row = g_row - pair_row(n)
            gram = _mm_nt(jnp.concatenate([k[rows, cols], q[rows, cols]], axis=0), _bd(bk[rows, cols]))
            kbk, qbk = gram[:CHUNK], gram[CHUNK:]
            items[(d, p, j)] = dict(
                at=at[rows, cols], rt=rt[rows, cols], v=v[rows, cols], bg=bg[rows, cols], kg=kg[rows, cols],
                dec=jnp.exp(gts[j][:, cols]),
                a_ab=-kbk * dec(gx_col, gx_row, strict), a_ak=kbk * dec(gx_col, g_row, strict),
                a_rb=-qbk * dec(g_col, gx_row, incl), a_rk=qbk * dec(g_col, g_row, incl))
    return items


def _ada_kernel(cond_ref, w_ref, b_ref, o_ref):
    cnd = cond_ref[...]
    o_ref[...] = _mm_hi(_silu(cnd), w_ref[...]) + b_ref[...]


def _ada_modulation(cond, ada_w, ada_b):
    depth, d, n6 = ada_w.shape
    tn = n6 // 4
    return pl.pallas_call(
        _ada_kernel,
        grid=(depth, n6 // tn),
        in_specs=[pl.BlockSpec((SUBLANES, d), lambda l, j: (0, 0)),
                  pl.BlockSpec((None, d, tn), lambda l, j: (l, 0, j)),
                  pl.BlockSpec((None, 1, tn), lambda l, j: (l, 0, j))],
        out_specs=pl.BlockSpec((None, SUBLANES, tn), lambda l, j: (l, 0, j)),
        out_shape=jax.ShapeDtypeStruct((depth, SUBLANES, n6), F32),
        compiler_params=_cparams("parallel", "parallel"),
    )(cond, ada_w, ada_b.reshape(depth, 1, n6))


def _in_kernel(nt, splits, x_ref, xp_ref, xn_ref, mod_ref, g_ref, w_ref, mu_ref, kk_ref, conv_ref,
               rw_ref, kkn_ref, q_ref, g_out_ref, s5_ref, ba_ref):
    i = pl.program_id(1)
    in_rw, gd3, gd_w, s5_w, _ = splits
    rw_w = kk_ref.shape[-1]

    def norm_mod(x):
        ms = jnp.mean(x * x, axis=-1, keepdims=True)
        h = x * lax.rsqrt(ms + NORM_EPS) * g_ref[...]
        return h * (1.0 + mod_ref[1:2, :]) + mod_ref[0:1, :]

    z = _mm(norm_mod(x_ref[...]), w_ref[...])
    ns = in_rw + gd3
    zh = _mm(norm_mod(jnp.concatenate([xp_ref[...], xn_ref[...]], axis=0)), w_ref[:, :ns])
    prev_row = jnp.where(i > 0, zh[SUBLANES - 1:SUBLANES], 0.0)
    next_row = jnp.where(i < nt - 1, zh[SUBLANES:SUBLANES + 1], 0.0)
    zs = z[:, :ns]
    zp, zn = _shifted(zs, prev_row, next_row)

    zr = zs[:, :in_rw]
    zr = zr + (0.5 * (zp[:, :in_rw] + zn[:, :in_rw]) - zr) * mu_ref[...]
    hmean = _head_mean_matrix(rw_w)
    kk = zr[:, rw_w:2 * rw_w] * kk_ref[...]
    kkn_ref[...] = kk * lax.rsqrt(_mm_exact_rhs(kk * kk, hmean, STAT_TERMS) * HEAD_DIM + NORM_EPS)
    rw_ref[:, :3 * rw_w] = zr[:, :3 * rw_w]
    lora = zr[:, 3 * rw_w:3 * rw_w + LANES]
    is_decay = lax.broadcasted_iota(jnp.int32, lora.shape, 1) < RW_DECAY_LORA
    rw_ref[:, 3 * rw_w:3 * rw_w + LANES] = jnp.where(is_decay, jnp.tanh(lora), lora)
    rw_ref[:, 3 * rw_w + LANES:] = zr[:, 3 * rw_w + LANES:]

    sq = slice(in_rw, ns)
    xc = _silu(conv_ref[0:1, :] * zp[:, sq] + conv_ref[1:2, :] * zs[:, sq] + conv_ref[2:3, :] * zn[:, sq])
    q = xc[:, :gd_w]
    k = xc[:, gd_w:2 * gd_w]
    hmean = _head_mean_matrix(gd_w)
    q_ref[:, :gd_w] = q * lax.rsqrt(_mm_exact_rhs(q * q, hmean, STAT_TERMS) * HEAD_DIM + NORM_EPS) * (HEAD_DIM ** -0.5)
    q_ref[:, gd_w:2 * gd_w] = k * lax.rsqrt(_mm_exact_rhs(k * k, hmean, STAT_TERMS) * HEAD_DIM + NORM_EPS)
    q_ref[:, 2 * gd_w:] = xc[:, 2 * gd_w:]

    off = ns
    for o_ref, width in zip((g_out_ref, s5_ref, ba_ref), splits[2:]):
        o_ref[...] = z[:, off:off + width]
        off += width


def _in_projection(xs, mod, mod_row, gain, w_perm, splits, mu, kk, conv):
    bsz, n, d = xs.shape
    tm = min(512, n)
    nt = n // tm
    tiles = tm // SUBLANES
    total = sum(splits)
    widths = (splits[0], kk.shape[-1]) + tuple(splits[1:])
    const = lambda a: pl.BlockSpec(a.shape, lambda b, i, nd=a.ndim: (0,) * nd)
    return pl.pallas_call(
        functools.partial(_in_kernel, nt, splits),
        grid=(bsz, nt),
        in_specs=[pl.BlockSpec((None, tm, d), lambda b, i: (b, i, 0)),
                  pl.BlockSpec((None, SUBLANES, d), lambda b, i: (b, jnp.maximum(i * tiles - 1, 0), 0)),
                  pl.BlockSpec((None, SUBLANES, d), lambda b, i: (b, jnp.minimum((i + 1) * tiles, nt * tiles - 1), 0)),
                  pl.BlockSpec((None, 6, d), lambda b, i: (mod_row(b), 0, 0)),
                  pl.BlockSpec((1, d), lambda b, i: (0, 0)),
                  pl.BlockSpec((d, total), lambda b, i: (0, 0)),
                  const(mu), const(kk), const(conv)],
        out_specs=[pl.BlockSpec((None, tm, w), lambda b, i: (b, i, 0)) for w in widths],
        out_shape=[jax.ShapeDtypeStruct((bsz, n, w), F32) for w in widths],
        compiler_params=_cparams("parallel", "parallel"),
    )(xs, xs, xs, mod, gain, w_perm, mu, kk, conv)


def _block_rows(n):
    nb = CHUNKS_PER_STEP
    while n % (nb * CHUNK):
        nb //= 2
    return nb * CHUNK


def _dir_specs(width, nc, rows):
    return [pl.BlockSpec((None, rows, width), lambda b, i: (b, i, 0)),
            pl.BlockSpec((None, rows, width), lambda b, i: (b, nc - 1 - i, 0))]


def _rw_kernel(nc, heads,
               zf_ref, zb_ref, kkf_ref, kkb_ref,
               w0_ref, wup_ref, a0_ref, aup_ref, ka_ref, rk_ref, s0_ref,
               yf_ref, yb_ref, bonf_ref, bonb_ref, sfin_ref, s_scr):
    i = pl.program_id(1)
    w = heads * HEAD_DIM

    @pl.when(i == 0)
    def _():
        s_scr[...] = s0_ref[...]

    hmean = _head_mean_matrix(w)
    nb = zf_ref.shape[0] // CHUNK
    items = {}
    views = ((zf_ref, kkf_ref, bonf_ref), (zb_ref, kkb_ref, bonb_ref))
    for d, (z_ref, kk_ref, bon_ref) in enumerate(views):
        r = z_ref[:, :w]
        k = z_ref[:, w:2 * w]
        v = z_ref[:, 2 * w:3 * w]
        o = 3 * w
        wd_t = z_ref[:, o:o + RW_DECAY_LORA]
        ad = z_ref[:, o + RW_DECAY_LORA:o + RW_DECAY_LORA + RW_ICLR_LORA]
        kk = kk_ref[...]
        lw = -math.exp(-0.5) * _sigmoid(w0_ref[d:d + 1, :] + _mm(wd_t, wup_ref[d]))
        a = _sigmoid(a0_ref[d:d + 1, :] + _mm(ad, aup_ref[d]))
        kd = k * (1.0 + (a - 1.0) * ka_ref[...])
        bon_ref[...] = _mm_exact_rhs(r * kd * rk_ref[...], hmean, STAT_TERMS) * HEAD_DIM * v
        items.update(_dplr_items(d, heads, nb, r, lw, kd, v, -kk, kk * a))
    yf_ref[...], yb_ref[...] = _run_chains(items, heads // 2, nb, s_scr)

    @pl.when(i == nc - 1)
    def _():
        sfin_ref[...] = s_scr[...]


def _rwkv_scan(z_rw, kk, p, s0):
    bsz, n, zw = z_rw.shape
    rows = _block_rows(n)
    nc = n // rows
    w = kk.shape[-1]
    heads = w // HEAD_DIM
    full = lambda a: pl.BlockSpec(a.shape, lambda b, i, nd=a.ndim: (0,) * nd)
    params = [p['rw_w0'], p['rw_wup'], p['rw_a0'], p['rw_aup'], p['rw_ka'], p['rw_rk']]
    state_spec = pl.BlockSpec((2, None, heads // 2, LANES, LANES), lambda b, i: (0, b, 0, 0, 0))
    fwd, bwd = _dir_specs(w, nc, rows)
    tok = jax.ShapeDtypeStruct((bsz, n, w), F32)
    return pl.pallas_call(
        functools.partial(_rw_kernel, nc, heads),
        grid=(bsz, nc),
        in_specs=_dir_specs(zw, nc, rows) + [fwd, bwd] + [full(a) for a in params] + [state_spec],
        out_specs=[fwd, bwd, fwd, bwd, state_spec],
        out_shape=[tok, tok, tok, tok, jax.ShapeDtypeStruct(s0.shape, F32)],
        scratch_shapes=[pltpu.VMEM((2, heads // 2, LANES, LANES), F32)],
        compiler_params=_cparams("parallel", "arbitrary"),
    )(z_rw, z_rw, kk, kk, *params, s0)


def _gd_kernel(nc, heads,
               zf_ref, zb_ref, baf_ref, bab_ref, alog_ref, dtb_ref, s0_ref,
               of_ref, ob_ref, sfin_ref, s_scr):
    i = pl.program_id(1)
    w = heads * HEAD_DIM

    @pl.when(i == 0)
    def _():
        s_scr[...] = s0_ref[...]

    lane_eye = _tri_masks(LANES, False)[2].astype(F32)
    nb = zf_ref.shape[0] // CHUNK
    items = {}
    for d, (z_ref, ba_ref) in enumerate(((zf_ref, baf_ref), (zb_ref, bab_ref))):
        q = z_ref[:, :w]
        k = z_ref[:, w:2 * w]
        v = z_ref[:, 2 * w:]
        ba = ba_ref[...]
        beta_all = _sigmoid(ba)
        lg_all = -jnp.exp(alog_ref[...]) * _softplus(ba + dtb_ref[...])
        g_all = _mm_exact_lhs(_block_tri_masks(ba.shape[0], d == 1), lg_all, SUM_TERMS)
        rows_t = _mm_nt_exact_lhs(lane_eye, jnp.concatenate([g_all, lg_all], axis=0), SUM_TERMS)
        items.update(_gdn_items(d, heads, nb, q, k, v, beta_all, g_all, lg_all, rows_t))
    of_ref[...], ob_ref[...] = _run_chains(items, heads // 2, nb, s_scr)

    @pl.when(i == nc - 1)
    def _():
        sfin_ref[...] = s_scr[...]


def _gdn_scan(z_q, z_ba, p, s0):
    bsz, n, zw = z_q.shape
    rows = _block_rows(n)
    nc = n // rows
    w = zw // 3
    heads = w // HEAD_DIM
    full = lambda a: pl.BlockSpec(a.shape, lambda b, i, nd=a.ndim: (0,) * nd)
    pad = lambda t: jnp.pad(t.reshape(1, 2 * heads), ((0, 0), (2 * heads, LANES - 4 * heads)))
    params = [pad(p['gd_a_log']), pad(p['gd_dt_bias'])]
    state_spec = pl.BlockSpec((2, None, heads // 2, LANES, LANES), lambda b, i: (0, b, 0, 0, 0))
    tok = jax.ShapeDtypeStruct((bsz, n, w), F32)
    return pl.pallas_call(
        functools.partial(_gd_kernel, nc, heads),
        grid=(bsz, nc),
        in_specs=_dir_specs(zw, nc, rows) + _dir_specs(LANES, nc, rows) + [full(a) for a in params]
        + [state_spec],
        out_specs=_dir_specs(w, nc, rows) + [state_spec],
        out_shape=[tok, tok, jax.ShapeDtypeStruct(s0.shape, F32)],
        scratch_shapes=[pltpu.VMEM((2, heads // 2, LANES, LANES), F32)],
        compiler_params=_cparams("parallel", "arbitrary"),
    )(z_q, z_q, z_ba, z_ba, *params, s0)


def _s5_disc_kernel(lre_ref, lim_ref, ldt_ref, bre_ref, bim_ref, abr_ref, abi_ref, bbr_ref, bbi_ref):
    lre = lre_ref[...]
    lim = lim_ref[...]
    dt = jnp.exp(ldt_ref[...])
    mag = jnp.exp(lre * dt)
    abr = mag * jnp.cos(lim * dt)
    abi = mag * jnp.sin(lim * dt)
    den = lre * lre + lim * lim
    fr = ((abr - 1.0) * lre + abi * lim) / den
    fi = (abi * lre - (abr - 1.0) * lim) / den
    abr_ref[...] = abr
    abi_ref[...] = abi
    bbr_ref[...] = fr * bre_ref[...] - fi * bim_ref[...]
    bbi_ref[...] = fr * bim_ref[...] + fi * bre_ref[...]


def _s5_discretise(p):
    _, groups, state = p['s5_lam_re'].shape
    ch = p['s5_b_re'].shape[-1]
    rows = 2 * groups * ch
    spread = lambda t: jnp.broadcast_to(t[:, :, None, :], (2, groups, ch, state)).reshape(rows, state)
    lre = spread(p['s5_lam_re'])
    lim = spread(p['s5_lam_im'])
    ldt = spread(jnp.broadcast_to(p['s5_log_dt'][:, :, None], (2, groups, state)))
    bre = jnp.swapaxes(p['s5_b_re'], 2, 3).reshape(rows, state)
    bim = jnp.swapaxes(p['s5_b_im'], 2, 3).reshape(rows, state)
    out = jax.ShapeDtypeStruct((rows, state), F32)
    abr, abi, bbr, bbi = pl.pallas_call(_s5_disc_kernel, out_shape=[out] * 4)(lre, lim, ldt, bre, bim)
    eye = jnp.eye(groups, dtype=F32)
    chain = lambda t: jnp.repeat(t.reshape(2, groups, ch, state)[:, :, 0, :].reshape(2, groups * state),
                                 SUBLANES // 2, axis=0)

    def bmat(t):
        t = t.reshape(2, groups, ch, state)
        return (t[:, :, :, None, :] * eye[None, :, None, :, None]).reshape(2, groups * ch, groups * state)

    def cmat(t):
        t = jnp.swapaxes(t, 2, 3)
        return (t[:, :, :, None, :] * eye[None, :, None, :, None]).reshape(2, groups * state, groups * ch)

    b_all = jnp.concatenate([bmat(bbr), bmat(bbi)], axis=-1).astype(MXU_DTYPE)
    c_all = jnp.stack([cmat(p['s5_c_re']), cmat(p['s5_c_im'])], axis=1).astype(MXU_DTYPE)
    return chain(abr), chain(abi), b_all, c_all


def _s5_kernel(uf_ref, ub_ref, b_ref, c_ref, ar_ref, ai_ref, h0r_ref, h0i_ref,
               yf_ref, yb_ref, hfr_ref, hfi_ref, bur, bui, hr, hi):
    i = pl.program_id(0)
    bsz, tc, w = uf_ref.shape
    half = bsz * tc
    ns = ar_ref.shape[1]

    @pl.when(i == 0)
    def _():
        hr[...] = h0r_ref[...]
        hi[...] = h0i_ref[...]

    p = lax.broadcasted_iota(jnp.int32, (tc, tc), 0)
    q = lax.broadcasted_iota(jnp.int32, (tc, tc), 1)
    flip = p + q == tc - 1
    u_f = uf_ref[...].reshape(half, w)
    u_b = jnp.concatenate([_mm_exact_lhs(flip, ub_ref[b], SUM_TERMS) for b in range(bsz)], axis=0)
    bu_f = _mm(u_f, b_ref[0])
    bu_b = _mm(u_b, b_ref[1])
    nlb = ns // LANES
    for j in range(nlb):
        lo, hi_ = j * LANES, (j + 1) * LANES
        bur[j, :half] = bu_f[:, lo:hi_]
        bur[j, half:] = bu_b[:, lo:hi_]
        bui[j, :half] = bu_f[:, ns + lo:ns + hi_]
        bui[j, half:] = bu_b[:, ns + lo:ns + hi_]
    per_pass = S5_SCAN_LANES // LANES
    for first in range(0, nlb, per_pass):
        blocks = range(first, first + per_pass)
        ars = [ar_ref[:, j * LANES:(j + 1) * LANES] for j in blocks]
        ais = [ai_ref[:, j * LANES:(j + 1) * LANES] for j in blocks]

        def body(s, carry, blocks=blocks, ars=ars, ais=ais):
            at = pl.ds(s, SUBLANES, stride=tc)
            out = []
            for j, ar, ai, (h_r, h_i) in zip(blocks, ars, ais, carry):
                n_r = ar * h_r - ai * h_i + bur[j, at, :]
                n_i = ar * h_i + ai * h_r + bui[j, at, :]
                bur[j, at, :] = n_r
                bui[j, at, :] = n_i
                out.append((n_r, n_i))
            return tuple(out)

        init = tuple((hr[:, j * LANES:(j + 1) * LANES], hi[:, j * LANES:(j + 1) * LANES]) for j in blocks)
        final = lax.fori_loop(0, tc, body, init, unroll=8)
        for j, (h_r, h_i) in zip(blocks, final):
            hr[:, j * LANES:(j + 1) * LANES] = h_r
            hi[:, j * LANES:(j + 1) * LANES] = h_i
    h_re = jnp.concatenate([bur[j] for j in range(nlb)], axis=1)
    h_im = jnp.concatenate([bui[j] for j in range(nlb)], axis=1)
    y_f = _mm(h_re[:half], c_ref[0, 0]) - _mm(h_im[:half], c_ref[0, 1])
    y_b = _mm(h_re[half:], c_ref[1, 0]) - _mm(h_im[half:], c_ref[1, 1])
    yf_ref[...] = y_f.reshape(yf_ref.shape)
    for b in range(bsz):
        yb_ref[b] = _mm_exact_lhs(flip, y_b[b * tc:(b + 1) * tc], SUM_TERMS)
    hfr_ref[...] = hr[...]
    hfi_ref[...] = hi[...]


def _s5_scan(u, disc, h0, column_major):
    ar, ai, b_all, c_all = disc
    bsz, n, w = u.shape
    ns = ar.shape[1]
    if column_major:
        tc, nblk = n // GRID_W, GRID_W
        view = (bsz, tc, nblk * w)
        spec = lambda blk: pl.BlockSpec((bsz, tc, w), lambda i: (0, 0, blk(i)))
    else:
        tc = min(S5_BLOCK, n)
        nblk = n // tc
        view = (bsz, nblk, tc, w)
        spec = lambda blk: pl.BlockSpec((bsz, None, tc, w), lambda i: (0, blk(i), 0, 0))
    assert tc % SUBLANES == 0 and 2 * bsz == SUBLANES
    fwd, bwd = spec(lambda i: i), spec(lambda i: nblk - 1 - i)
    full = lambda a: pl.BlockSpec(a.shape, lambda i, nd=a.ndim: (0,) * nd)
    st = jax.ShapeDtypeStruct((SUBLANES, ns), F32)
    tok = jax.ShapeDtypeStruct(view, F32)
    uv = u.reshape(view)
    yf, yb, hfr, hfi = pl.pallas_call(
        _s5_kernel,
        grid=(nblk,),
        in_specs=[fwd, bwd, full(b_all), full(c_all), full(ar), full(ai), full(h0[0]), full(h0[1])],
        out_specs=[fwd, bwd, full(ar), full(ar)],
        out_shape=[tok, tok, st, st],
        scratch_shapes=[pltpu.VMEM((ns // LANES, SUBLANES * tc, LANES), F32),
                        pltpu.VMEM((ns // LANES, SUBLANES * tc, LANES), F32),
                        pltpu.VMEM((SUBLANES, ns), F32), pltpu.VMEM((SUBLANES, ns), F32)],
        compiler_params=_cparams("arbitrary"),
    )(uv, uv, b_all, c_all, ar, ai, h0[0], h0[1])
    return yf.reshape(u.shape), yb.reshape(u.shape), (hfr, hfi)


def _mix_residual(x_ref, mod_ref, gd_ref, yf_ref, yb_ref, bonf_ref, bonb_ref, of_ref, ob_ref, zg_ref,
                  s5f_ref, s5b_ref, u5_ref, gup_ref, lnxg_ref, lnxb_ref, gdg_ref, s5d_ref, gluw_ref, glub_ref,
                  wout_ref):
    hmean = _head_mean_matrix(yf_ref.shape[-1])
    y = yf_ref[...] + yb_ref[...]
    yc = y - _mm_exact_rhs(y, hmean, STAT_TERMS)
    yn = yc * lax.rsqrt(_mm_exact_rhs(yc * yc, hmean, STAT_TERMS) + LNX_EPS)
    gate = _mm(_sigmoid(gd_ref[...]), gup_ref[...])
    y_rw = (yn * lnxg_ref[...] + lnxb_ref[...] + bonf_ref[...] + bonb_ref[...]) * gate
    o = of_ref[...] + ob_ref[...]
    o = o * lax.rsqrt(_mm_exact_rhs(o * o, hmean, STAT_TERMS) + NORM_EPS) * gdg_ref[...]
    y_gd = o * _silu(zg_ref[...])
    y5 = _gelu_tanh(s5f_ref[...] + s5b_ref[...] + s5d_ref[...] * u5_ref[...])
    y5 = y5 * _sigmoid(_mm(y5, gluw_ref[...]) + glub_ref[...])
    mix = jnp.concatenate([y_rw, y_gd, y5], axis=1)
    return x_ref[...] + mod_ref[2:3, :] * _mm(mix, wout_ref[...])


def _mix_ffn_kernel(hidden, final, n_mix, *refs):
    mix_refs, (g_ref, wgu_ref, wd_ref, fg_ref, o_ref) = refs[:n_mix], refs[n_mix:]
    mod_ref = mix_refs[1]
    x = _mix_residual(*mix_refs)
    ms = jnp.mean(x * x, axis=-1, keepdims=True)
    h = x * lax.rsqrt(ms + NORM_EPS) * g_ref[...]
    h = h * (1.0 + mod_ref[4:5, :]) + mod_ref[3:4, :]
    gu = _mm(h, wgu_ref[...])
    act = _silu(gu[:, :hidden]) * gu[:, hidden:]
    o = x + mod_ref[5:6, :] * _mm(act, wd_ref[...])
    if final:
        ms = jnp.mean(o * o, axis=-1, keepdims=True)
        o = o * lax.rsqrt(ms + NORM_EPS) * fg_ref[...]
    o_ref[...] = o


def _mix_ffn(xs, mod, mod_row, z_rw, toks, params, gain, w_gu, w_down, final_gain, final):
    bsz, n, d = xs.shape
    hidden = w_down.shape[0]
    tm = min(256, n)
    tok_spec = lambda a: pl.BlockSpec((None, tm, a.shape[-1]), lambda b, i: (b, i, 0))
    const = lambda a: pl.BlockSpec(a.shape, lambda b, i, nd=a.ndim: (0,) * nd,
                                   pipeline_mode=pl.Buffered(1))
    gate_in = z_rw.shape[-1] // RW_GATE_LORA - 1
    assert RW_GATE_LORA == LANES and z_rw.shape[-1] % LANES == 0
    mix_specs = ([tok_spec(xs), pl.BlockSpec((None, 6, d), lambda b, i: (mod_row(b), 0, 0)),
                  pl.BlockSpec((None, tm, LANES), lambda b, i: (b, i, gate_in))]
                 + [tok_spec(a) for a in toks] + [const(a) for a in params])
    return pl.pallas_call(
        functools.partial(_mix_ffn_kernel, hidden, final, len(mix_specs)),
        grid=(bsz, n // tm),
        in_specs=mix_specs + [const(gain), const(w_gu), const(w_down), const(final_gain)],
        out_specs=tok_spec(xs),
        out_shape=jax.ShapeDtypeStruct(xs.shape, F32),
        compiler_params=_cparams("parallel", "parallel"),
    )(xs, mod, z_rw, *toks, *params, gain, w_gu, w_down, final_gain)


def kernel(x, c, ctx, c_ctx, norm1_g, norm2_g, final_g, ada_w, ada_b, w_in, w_out,
           rw_mu, rw_w0, rw_wup, rw_a0, rw_aup, rw_gup, rw_kk, rw_ka, rw_rk, rw_lnx_g, rw_lnx_b,
           gd_conv, gd_a_log, gd_dt_bias, gd_norm_g,
           s5_lam_re, s5_lam_im, s5_log_dt, s5_b_re, s5_b_im, s5_c_re, s5_c_im, s5_d, s5_glu_w, s5_glu_b,
           ffn_w_gate, ffn_w_up, ffn_w_down):
    bsz, n, d = x.shape
    depth = w_in.shape[0]
    rw_w = rw_kk.shape[-1]
    gd_w = gd_conv.shape[-1] // 3
    s5_w = s5_d.shape[-1]
    rw_heads = rw_w // HEAD_DIM
    gd_heads = gd_w // HEAD_DIM
    in_rw = rw_mu.shape[-1]
    assert 2 * bsz == SUBLANES and n % GRID_W == 0 and n % CHUNK == 0 and ctx.shape[1] % CHUNK == 0
    assert rw_heads % 2 == 0 and gd_heads % 2 == 0 and 2 * HEAD_DIM == LANES

    cond = jnp.concatenate([c, c_ctx[None, :], jnp.zeros((SUBLANES - bsz - 1, d), F32)], axis=0)
    mod_all = _ada_modulation(cond, ada_w, ada_b).reshape(depth, SUBLANES, 6, d)
    x_row = lambda b: b
    ctx_row = lambda b: bsz

    o_gd = in_rw
    o_ba = o_gd + 3 * gd_w
    o_gate = o_ba + 4 * gd_heads
    o_s5 = o_gate + gd_w
    splits = (in_rw, 3 * gd_w, gd_w, s5_w, LANES)
    w_perm = jnp.concatenate(
        [w_in[:, :, :o_gd], w_in[:, :, o_gd:o_ba], w_in[:, :, o_gate:o_s5], w_in[:, :, o_s5:],
         w_in[:, :, o_ba:o_gate], jnp.zeros((depth, d, LANES - 4 * gd_heads), F32)], axis=-1).astype(MXU_DTYPE)
    w_gu = jnp.concatenate([ffn_w_gate, ffn_w_up], axis=-1).astype(MXU_DTYPE)
    w_dn = ffn_w_down.astype(MXU_DTYPE)
    w_o = w_out.astype(MXU_DTYPE)
    glu_w = s5_glu_w.astype(MXU_DTYPE)
    row = lambda t: t.reshape(1, -1)

    for l in range(depth):
        p = {'rw_mu': row(rw_mu[l]), 'rw_w0': rw_w0[l], 'rw_wup': rw_wup[l], 'rw_a0': rw_a0[l],
             'rw_aup': rw_aup[l], 'rw_gup': rw_gup[l], 'rw_kk': row(rw_kk[l]), 'rw_ka': row(rw_ka[l]),
             'rw_rk': row(rw_rk[l]), 'gd_conv': gd_conv[l], 'gd_a_log': gd_a_log[l],
             'gd_dt_bias': gd_dt_bias[l], 's5_lam_re': s5_lam_re[l], 's5_lam_im': s5_lam_im[l],
             's5_log_dt': s5_log_dt[l], 's5_b_re': s5_b_re[l], 's5_b_im': s5_b_im[l],
             's5_c_re': s5_c_re[l], 's5_c_im': s5_c_im[l]}
        mix_params = [rw_gup[l], row(rw_lnx_g[l]), row(rw_lnx_b[l]), row(jnp.tile(gd_norm_g[l], gd_heads)),
                      row(s5_d[l]), glu_w[l], row(s5_glu_b[l]), w_o[l]]
        disc = _s5_discretise(p)
        mod = mod_all[l]
        states = (jnp.zeros((2, bsz, rw_heads // 2, LANES, LANES), F32),
                  jnp.zeros((2, bsz, gd_heads // 2, LANES, LANES), F32),
                  (jnp.zeros((SUBLANES, disc[0].shape[1]), F32),) * 2)
        new_streams = []
        for stream, mod_row, column_major, is_ctx in ((ctx, ctx_row, False, True), (x, x_row, True, False)):
            z_rw, kk, z_q, z_g, z_s5, z_ba = _in_projection(stream, mod, mod_row, row(norm1_g[l]), w_perm[l], splits,
                                                            p['rw_mu'], p['rw_kk'], p['gd_conv'])
            yf, yb, bonf, bonb, s_rw = _rwkv_scan(z_rw, kk, p, states[0])
            of, ob, s_gd = _gdn_scan(z_q, z_ba, p, states[1])
            s5f, s5b, s_s5 = _s5_scan(z_s5, disc, states[2], column_major)
            states = (s_rw, s_gd, s_s5)
            if is_ctx and l == depth - 1:
                new_streams.append(stream)
                continue
            final = (not is_ctx) and l == depth - 1
            new_streams.append(_mix_ffn(stream, mod, mod_row, z_rw, [yf, yb, bonf, bonb, of, ob, z_g, s5f, s5b, z_s5],
                                        mix_params, row(norm2_g[l]), w_gu[l], w_dn[l], row(final_g), final))
        ctx, x = new_streams
    return x
```

```python
import functools
import math

import jax
import jax.numpy as jnp
from jax import lax
from jax.experimental import pallas as pl
from jax.experimental.pallas import tpu as pltpu

F32 = jnp.float32
MXU_DTYPE = jnp.bfloat16
HIGHEST = lax.Precision.HIGHEST

NORM_EPS = 1e-6
LNX_EPS = 64e-5
HEAD_DIM = 64
GRID_W = 64
CHUNK = 64
CHUNKS_PER_STEP = 4
STAT_TERMS = 2
SUM_TERMS = 3
SUBLANES = 8
LANES = 128
RW_DECAY_LORA = 64
RW_ICLR_LORA = 64
RW_GATE_LORA = 128
S5_BLOCK = 64
S5_SCAN_LANES = 512
VMEM_LIMIT = 56 * 1024 * 1024


def _cparams(*sem):
    return pltpu.CompilerParams(dimension_semantics=sem, vmem_limit_bytes=VMEM_LIMIT)


def _sigmoid(x):
    return 1.0 / (1.0 + jnp.exp(-x))


def _silu(x):
    return x * _sigmoid(x)


def _softplus(x):
    return jnp.maximum(x, 0.0) + jnp.log(1.0 + jnp.exp(-jnp.abs(x)))


def _gelu_tanh(x):
    return 0.5 * x * (1.0 + jnp.tanh(0.7978845608028654 * (x + 0.044715 * x * x * x)))


def _mm(a, b):
    return jnp.dot(a.astype(MXU_DTYPE), b.astype(MXU_DTYPE), preferred_element_type=F32)


def _mm_nt(a, b):
    return lax.dot_general(a.astype(MXU_DTYPE), b.astype(MXU_DTYPE), (((1,), (1,)), ((), ())),
                           preferred_element_type=F32)


def _mm_tn(a, b):
    return lax.dot_general(a.astype(MXU_DTYPE), b.astype(MXU_DTYPE), (((0,), (0,)), ((), ())),
                           preferred_element_type=F32)


def _mm_hi(a, b):
    return jnp.dot(a, b, precision=HIGHEST, preferred_element_type=F32)


def _split(x, terms):
    parts = []
    for _ in range(terms):
        p = x.astype(MXU_DTYPE)
        parts.append(p)
        x = x - p.astype(F32)
    return parts


def _mm_exact_rhs(a, e, terms):
    e = e.astype(MXU_DTYPE)
    return sum(jnp.dot(p, e, preferred_element_type=F32) for p in _split(a, terms))


def _mm_exact_lhs(e, b, terms):
    e = e.astype(MXU_DTYPE)
    return sum(jnp.dot(e, p, preferred_element_type=F32) for p in _split(b, terms))


def _mm_nt_exact_lhs(e, b, terms):
    e = e.astype(MXU_DTYPE)
    return sum(lax.dot_general(e, p, (((1,), (1,)), ((), ())), preferred_element_type=F32)
               for p in _split(b, terms))


def _head_mean_matrix(width):
    r = lax.broadcasted_iota(jnp.int32, (width, width), 0) // HEAD_DIM
    c = lax.broadcasted_iota(jnp.int32, (width, width), 1) // HEAD_DIM
    return jnp.where(r == c, 1.0 / HEAD_DIM, 0.0).astype(F32)


def _tri_masks(n, reverse):
    t = lax.broadcasted_iota(jnp.int32, (n, n), 0)
    j = lax.broadcasted_iota(jnp.int32, (n, n), 1)
    if reverse:
        return j >= t, j > t, t == j
    return j <= t, j < t, t == j


def _shifted(z, prev_row, next_row):
    n = z.shape[0]
    row = lax.broadcasted_iota(jnp.int32, z.shape, 0)
    zp = jnp.where(row == 0, prev_row, pltpu.roll(z, 1, axis=0))
    zn = jnp.where(row == n - 1, next_row, pltpu.roll(z, n - 1, axis=0))
    return zp, zn


def _block_tri_masks(n, reverse):
    t = lax.broadcasted_iota(jnp.int32, (n, n), 0)
    j = lax.broadcasted_iota(jnp.int32, (n, n), 1)
    order = (j >= t) if reverse else (j <= t)
    return order & (t // CHUNK == j // CHUNK)


def _pair_index(shape):
    t = lax.broadcasted_iota(jnp.int32, shape, 0)
    j = lax.broadcasted_iota(jnp.int32, shape, 1) % HEAD_DIM
    return t, j


def _pair_tri_masks(reverse):
    t, j = _pair_index((CHUNK, 2 * CHUNK))
    if reverse:
        return j >= t, j > t
    return j <= t, j < t


def _bd(x):
    x = x.astype(MXU_DTYPE)
    lo = lax.broadcasted_iota(jnp.int32, x.shape, 1) < HEAD_DIM
    zero = jnp.zeros_like(x)
    return jnp.concatenate([jnp.where(lo, x, zero), jnp.where(lo, zero, x)], axis=0)


def _bd_mask(n):
    r = lax.broadcasted_iota(jnp.int32, (n, n), 0) // HEAD_DIM
    c = lax.broadcasted_iota(jnp.int32, (n, n), 1) // HEAD_DIM
    return r == c


def _unit_triangular_inverses(nmats):
    c = nmats[0].shape[0]
    t, j = _pair_index(nmats[0].shape)
    eye = (t == j).astype(F32)
    s = SUBLANES
    same = t // s == j // s
    ps = [jnp.where(same, m, 0.0) for m in nmats]
    xs = [eye + p for p in ps]
    for _ in range(2):
        ps = [_mm(p, _bd(p)) for p in ps]
        xs = [x + _mm(p, _bd(x)) for p, x in zip(ps, xs)]
    while s < c:
        sel = (t // (2 * s) == j // (2 * s)) & (t // s != j // s)
        cx = [_mm(jnp.where(sel, m, 0.0), _bd(x)) for m, x in zip(nmats, xs)]
        xs = [x + _mm(x, _bd(y)) for x, y in zip(xs, cx)]
        s *= 2
    return xs


def _scan_state_free_stage(items):
    c, n = items[0]['v'].shape
    keep = _bd_mask(n)
    xs = _unit_triangular_inverses([it['a_ab'] for it in items])
    av = [_mm(jnp.concatenate([it['a_ak'], it['a_rk']], axis=0), _bd(it['v'])) for it in items]
    ws = [_mm(x, jnp.concatenate([_bd(it['at']), _bd(a[:c])], axis=1)) for x, it, a in zip(xs, items, av)]
    kv = [jnp.where(keep, _mm_tn(it['v'], it['kg']), 0.0) for it in items]
    for it, w, a, m in zip(items, ws, av, kv):
        it['lhs'] = jnp.concatenate([w[:, :n], it['rt']], axis=0)
        it['p2'] = w[:, n:]
        it['y0'] = a[c:]
        it['kv'] = m


def _scan_state_stage(items, states):
    c, n = items[0]['p2'].shape
    keep = _bd_mask(n)
    ps = [_mm_nt(it['lhs'], s) for it, s in zip(items, states)]
    us = [p[:c] + it['p2'] for p, it in zip(ps, items)]
    ys = [p[c:] + _mm(it['a_rb'], _bd(u)) + it['y0'] for p, it, u in zip(ps, items, us)]
    new = [s * it['dec'] + jnp.where(keep, _mm_tn(u, it['bg']), 0.0) + it['kv']
           for s, it, u in zip(states, items, us)]
    return ys, new


def _run_chains(items, pairs, nb, s_scr):
    _scan_state_free_stage(list(items.values()))
    chains = [(d, p) for d in range(2) for p in range(pairs)]
    states = [s_scr[d, p] for d, p in chains]
    ys = {}
    for step in range(nb):
        keys = [(d, p, nb - 1 - step if d else step) for d, p in chains]
        out, states = _scan_state_stage([items[k] for k in keys], states)
        ys.update(zip(keys, out))
    for (d, p), s in zip(chains, states):
        s_scr[d, p] = s
    return [jnp.concatenate([jnp.concatenate([ys[(d, p, j)] for p in range(pairs)], axis=1)
                             for j in range(nb)], axis=0) for d in range(2)]


def _dplr_items(d, heads, nb, r, lw, k, v, a, b):
    n, w = r.shape
    incl, strict = _pair_tri_masks(d == 1)
    g = _mm_exact_lhs(_block_tri_masks(n, d == 1), lw, SUM_TERMS)
    gts = [jnp.sum(lw[j * CHUNK:(j + 1) * CHUNK], axis=0, keepdims=True) for j in range(nb)]
    gtb = jnp.concatenate([jnp.broadcast_to(gt, (CHUNK, w)) for gt in gts], axis=0)
    e_neg = jnp.exp(-g)
    e_rem = jnp.exp(gtb - g)
    rt = r * jnp.exp(g)
    at = a * jnp.exp(g - lw)
    bt = b * e_neg
    kt = k * e_neg
    bg = b * e_rem
    kg = k * e_rem
    pw = 2 * CHUNK
    items = {}
    for p in range(heads // 2):
        for j in range(nb):
            rows = slice(j * CHUNK, (j + 1) * CHUNK)
            cols = slice(p * LANES, (p + 1) * LANES)
            am = _mm_nt(jnp.concatenate([at[rows, cols], rt[rows, cols]], axis=0),
                        jnp.concatenate([_bd(bt[rows, cols]), _bd(kt[rows, cols])], axis=0))
            items[(d, p, j)] = dict(
                at=at[rows, cols], rt=rt[rows, cols], v=v[rows, cols], bg=bg[rows, cols], kg=kg[rows, cols],
                dec=jnp.exp(gts[j][:, cols]),
                a_ab=jnp.where(strict, am[:CHUNK, :pw], 0.0), a_ak=jnp.where(strict, am[:CHUNK, pw:], 0.0),
                a_rb=jnp.where(incl, am[CHUNK:, :pw], 0.0), a_rk=jnp.where(incl, am[CHUNK:, pw:], 0.0))
    return items


def _gdn_items(d, heads, nb, q, k, v, beta_all, g_all, lg_all, rows_t):
    n, w = q.shape
    assert CHUNK == HEAD_DIM
    incl, strict = _pair_tri_masks(d == 1)

    def dec(col, row, mask):
        return jnp.where(mask, jnp.exp(jnp.minimum(col - row, 0.0)), 0.0)

    src = lax.broadcasted_iota(jnp.int32, (LANES, w), 0)
    head = lax.broadcasted_iota(jnp.int32, (LANES, w), 1) // HEAD_DIM + d * heads
    beta = _mm_exact_rhs(beta_all, src == head, STAT_TERMS)
    gl = _mm_exact_rhs(jnp.concatenate([g_all, lg_all], axis=0), src == head + 2 * heads, SUM_TERMS)
    g, lg = gl[:n], gl[n:]
    gx = g - lg
    gts = [jnp.sum(lg[j * CHUNK:(j + 1) * CHUNK], axis=0, keepdims=True) for j in range(nb)]
    gtb = jnp.concatenate([jnp.broadcast_to(gt, (CHUNK, w)) for gt in gts], axis=0)
    bk = k * beta
    at = k * jnp.exp(gx)
    rt = q * jnp.exp(g)
    bg = -bk * jnp.exp(gtb - gx)
    kg = bk * jnp.exp(gtb - g)
    items = {}
    for p in range(heads // 2):
        cg = (2 + d) * heads + 2 * p
        for j in range(nb):
            rows = slice(j * CHUNK, (j + 1) * CHUNK)
            cols = slice(p * LANES, (p + 1) * LANES)
            g_col, gx_col = g[rows, cols], gx[rows, cols]
            pair_row = lambda o: jnp.concatenate([rows_t[cg:cg + 1, o + j * CHUNK:o + (j + 1) * CHUNK],
                                                  rows_t[cg + 1:cg + 2, o + j * CHUNK:o + (j + 1) * CHUNK]], axis=1)
            g_row = pair_row(0)
            gx_row = g_row - pair_row(n)
            gram = _mm_nt(jnp.concatenate([k[rows, cols], q[rows, cols]], axis=0), _bd(bk[rows, cols]))
            kbk, qbk = gram[:CHUNK], gram[CHUNK:]
            items[(d, p, j)] = dict(
                at=at[rows, cols], rt=rt[rows, cols], v=v[rows, cols], bg=bg[rows, cols], kg=kg[rows, cols],
                dec=jnp.exp(gts[j][:, cols]),
                a_ab=-kbk * dec(gx_col, gx_row, strict), a_ak=kbk * dec(gx_col, g_row, strict),
                a_rb=-qbk * dec(g_col, gx_row, incl), a_rk=qbk * dec(g_col, g_row, incl))
    return items


def _ada_kernel(cond_ref, w_ref, b_ref, o_ref):
    cnd = cond_ref[...]
    o_ref[...] = _mm_hi(_silu(cnd), w_ref[...]) + b_ref[...]


def _ada_modulation(cond, ada_w, ada_b):
    depth, d, n6 = ada_w.shape
    tn = n6 // 4
    return pl.pallas_call(
        _ada_kernel,
        grid=(depth, n6 // tn),
        in_specs=[pl.BlockSpec((SUBLANES, d), lambda l, j: (0, 0)),
                  pl.BlockSpec((None, d, tn), lambda l, j: (l, 0, j)),
                  pl.BlockSpec((None, 1, tn), lambda l, j: (l, 0, j))],
        out_specs=pl.BlockSpec((None, SUBLANES, tn), lambda l, j: (l, 0, j)),
        out_shape=jax.ShapeDtypeStruct((depth, SUBLANES, n6), F32),
        compiler_params=_cparams("parallel", "parallel"),
    )(cond, ada_w, ada_b.reshape(depth, 1, n6))


def _in_kernel(nt, splits, x_ref, xp_ref, xn_ref, mod_ref, g_ref, w_ref, mu_ref, kk_ref, conv_ref,
               rw_ref, kkn_ref, q_ref, g_out_ref, s5_ref, ba_ref):
    i = pl.program_id(1)
    in_rw, gd3, gd_w, s5_w, _ = splits
    rw_w = kk_ref.shape[-1]

    def norm_mod(x):
        ms = jnp.mean(x * x, axis=-1, keepdims=True)
        h = x * lax.rsqrt(ms + NORM_EPS) * g_ref[...]
        return h * (1.0 + mod_ref[1:2, :]) + mod_ref[0:1, :]

    z = _mm(norm_mod(x_ref[...]), w_ref[...])
    ns = in_rw + gd3
    zh = _mm(norm_mod(jnp.concatenate([xp_ref[...], xn_ref[...]], axis=0)), w_ref[:, :ns])
    prev_row = jnp.where(i > 0, zh[SUBLANES - 1:SUBLANES], 0.0)
    next_row = jnp.where(i < nt - 1, zh[SUBLANES:SUBLANES + 1], 0.0)
    zs = z[:, :ns]
    zp, zn = _shifted(zs, prev_row, next_row)

    zr = zs[:, :in_rw]
    zr = zr + (0.5 * (zp[:, :in_rw] + zn[:, :in_rw]) - zr) * mu_ref[...]
    hmean = _head_mean_matrix(rw_w)
    kk = zr[:, rw_w:2 * rw_w] * kk_ref[...]
    kkn_ref[...] = kk * lax.rsqrt(_mm_exact_rhs(kk * kk, hmean, STAT_TERMS) * HEAD_DIM + NORM_EPS)
    rw_ref[:, :3 * rw_w] = zr[:, :3 * rw_w]
    lora = zr[:, 3 * rw_w:3 * rw_w + LANES]
    is_decay = lax.broadcasted_iota(jnp.int32, lora.shape, 1) < RW_DECAY_LORA
    rw_ref[:, 3 * rw_w:3 * rw_w + LANES] = jnp.where(is_decay, jnp.tanh(lora), lora)
    rw_ref[:, 3 * rw_w + LANES:] = zr[:, 3 * rw_w + LANES:]

    sq = slice(in_rw, ns)
    xc = _silu(conv_ref[0:1, :] * zp[:, sq] + conv_ref[1:2, :] * zs[:, sq] + conv_ref[2:3, :] * zn[:, sq])
    q = xc[:, :gd_w]
    k = xc[:, gd_w:2 * gd_w]
    hmean = _head_mean_matrix(gd_w)
    q_ref[:, :gd_w] = q * lax.rsqrt(_mm_exact_rhs(q * q, hmean, STAT_TERMS) * HEAD_DIM + NORM_EPS) * (HEAD_DIM ** -0.5)
    q_ref[:, gd_w:2 * gd_w] = k * lax.rsqrt(_mm_exact_rhs(k * k, hmean, STAT_TERMS) * HEAD_DIM + NORM_EPS)
    q_ref[:, 2 * gd_w:] = xc[:, 2 * gd_w:]

    off = ns
    for o_ref, width in zip((g_out_ref, s5_ref, ba_ref), splits[2:]):
        o_ref[...] = z[:, off:off + width]
        off += width


def _in_projection(xs, mod, mod_row, gain, w_perm, splits, mu, kk, conv):
    bsz, n, d = xs.shape
    tm = min(512, n)
    nt = n // tm
    tiles = tm // SUBLANES
    total = sum(splits)
    widths = (splits[0], kk.shape[-1]) + tuple(splits[1:])
    const = lambda a: pl.BlockSpec(a.shape, lambda b, i, nd=a.ndim: (0,) * nd)
    return pl.pallas_call(
        functools.partial(_in_kernel, nt, splits),
        grid=(bsz, nt),
        in_specs=[pl.BlockSpec((None, tm, d), lambda b, i: (b, i, 0)),
                  pl.BlockSpec((None, SUBLANES, d), lambda b, i: (b, jnp.maximum(i * tiles - 1, 0), 0)),
                  pl.BlockSpec((None, SUBLANES, d), lambda b, i: (b, jnp.minimum((i + 1) * tiles, nt * tiles - 1), 0)),
                  pl.BlockSpec((None, 6, d), lambda b, i: (mod_row(b), 0, 0)),
                  pl.BlockSpec((1, d), lambda b, i: (0, 0)),
                  pl.BlockSpec((d, total), lambda b, i: (0, 0)),
                  const(mu), const(kk), const(conv)],
        out_specs=[pl.BlockSpec((None, tm, w), lambda b, i: (b, i, 0)) for w in widths],
        out_shape=[jax.ShapeDtypeStruct((bsz, n, w), F32) for w in widths],
        compiler_params=_cparams("parallel", "parallel"),
    )(xs, xs, xs, mod, gain, w_perm, mu, kk, conv)


def _block_rows(n):
    nb = CHUNKS_PER_STEP
    while n % (nb * CHUNK):
        nb //= 2
    return nb * CHUNK


def _dir_specs(width, nc, rows):
    return [pl.BlockSpec((None, rows, width), lambda b, i: (b, i, 0)),
            pl.BlockSpec((None, rows, width), lambda b, i: (b, nc - 1 - i, 0))]


def _rw_kernel(nc, heads,
               zf_ref, zb_ref, kkf_ref, kkb_ref,
               w0_ref, wup_ref, a0_ref, aup_ref, ka_ref, rk_ref, s0_ref,
               yf_ref, yb_ref, bonf_ref, bonb_ref, sfin_ref, s_scr):
    i = pl.program_id(1)
    w = heads * HEAD_DIM

    @pl.when(i == 0)
    def _():
        s_scr[...] = s0_ref[...]

    hmean = _head_mean_matrix(w)
    nb = zf_ref.shape[0] // CHUNK
    items = {}
    views = ((zf_ref, kkf_ref, bonf_ref), (zb_ref, kkb_ref, bonb_ref))
    for d, (z_ref, kk_ref, bon_ref) in enumerate(views):
        r = z_ref[:, :w]
        k = z_ref[:, w:2 * w]
        v = z_ref[:, 2 * w:3 * w]
        o = 3 * w
        wd_t = z_ref[:, o:o + RW_DECAY_LORA]
        ad = z_ref[:, o + RW_DECAY_LORA:o + RW_DECAY_LORA + RW_ICLR_LORA]
        kk = kk_ref[...]
        lw = -math.exp(-0.5) * _sigmoid(w0_ref[d:d + 1, :] + _mm(wd_t, wup_ref[d]))
        a = _sigmoid(a0_ref[d:d + 1, :] + _mm(ad, aup_ref[d]))
        kd = k * (1.0 + (a - 1.0) * ka_ref[...])
        bon_ref[...] = _mm_exact_rhs(r * kd * rk_ref[...], hmean, STAT_TERMS) * HEAD_DIM * v
        items.update(_dplr_items(d, heads, nb, r, lw, kd, v, -kk, kk * a))
    yf_ref[...], yb_ref[...] = _run_chains(items, heads // 2, nb, s_scr)

    @pl.when(i == nc - 1)
    def _():
        sfin_ref[...] = s_scr[...]


def _rwkv_scan(z_rw, kk, p, s0):
    bsz, n, zw = z_rw.shape
    rows = _block_rows(n)
    nc = n // rows
    w = kk.shape[-1]
    heads = w // HEAD_DIM
    full = lambda a: pl.BlockSpec(a.shape, lambda b, i, nd=a.ndim: (0,) * nd)
    params = [p['rw_w0'], p['rw_wup'], p['rw_a0'], p['rw_aup'], p['rw_ka'], p['rw_rk']]
    state_spec = pl.BlockSpec((2, None, heads // 2, LANES, LANES), lambda b, i: (0, b, 0, 0, 0))
    fwd, bwd = _dir_specs(w, nc, rows)
    tok = jax.ShapeDtypeStruct((bsz, n, w), F32)
    return pl.pallas_call(
        functools.partial(_rw_kernel, nc, heads),
        grid=(bsz, nc),
        in_specs=_dir_specs(zw, nc, rows) + [fwd, bwd] + [full(a) for a in params] + [state_spec],
        out_specs=[fwd, bwd, fwd, bwd, state_spec],
        out_shape=[tok, tok, tok, tok, jax.ShapeDtypeStruct(s0.shape, F32)],
        scratch_shapes=[pltpu.VMEM((2, heads // 2, LANES, LANES), F32)],
        compiler_params=_cparams("parallel", "arbitrary"),
    )(z_rw, z_rw, kk, kk, *params, s0)


def _gd_kernel(nc, heads,
               zf_ref, zb_ref, baf_ref, bab_ref, alog_ref, dtb_ref, s0_ref,
               of_ref, ob_ref, sfin_ref, s_scr):
    i = pl.program_id(1)
    w = heads * HEAD_DIM

    @pl.when(i == 0)
    def _():
        s_scr[...] = s0_ref[...]

    lane_eye = _tri_masks(LANES, False)[2].astype(F32)
    nb = zf_ref.shape[0] // CHUNK
    items = {}
    for d, (z_ref, ba_ref) in enumerate(((zf_ref, baf_ref), (zb_ref, bab_ref))):
        q = z_ref[:, :w]
        k = z_ref[:, w:2 * w]
        v = z_ref[:, 2 * w:]
        ba = ba_ref[...]
        beta_all = _sigmoid(ba)
        lg_all = -jnp.exp(alog_ref[...]) * _softplus(ba + dtb_ref[...])
        g_all = _mm_exact_lhs(_block_tri_masks(ba.shape[0], d == 1), lg_all, SUM_TERMS)
        rows_t = _mm_nt_exact_lhs(lane_eye, jnp.concatenate([g_all, lg_all], axis=0), SUM_TERMS)
        items.update(_gdn_items(d, heads, nb, q, k, v, beta_all, g_all, lg_all, rows_t))
    of_ref[...], ob_ref[...] = _run_chains(items, heads // 2, nb, s_scr)

    @pl.when(i == nc - 1)
    def _():
        sfin_ref[...] = s_scr[...]


def _gdn_scan(z_q, z_ba, p, s0):
    bsz, n, zw = z_q.shape
    rows = _block_rows(n)
    nc = n // rows
    w = zw // 3
    heads = w // HEAD_DIM
    full = lambda a: pl.BlockSpec(a.shape, lambda b, i, nd=a.ndim: (0,) * nd)
    pad = lambda t: jnp.pad(t.reshape(1, 2 * heads), ((0, 0), (2 * heads, LANES - 4 * heads)))
    params = [pad(p['gd_a_log']), pad(p['gd_dt_bias'])]
    state_spec = pl.BlockSpec((2, None, heads // 2, LANES, LANES), lambda b, i: (0, b, 0, 0, 0))
    tok = jax.ShapeDtypeStruct((bsz, n, w), F32)
    return pl.pallas_call(
        functools.partial(_gd_kernel, nc, heads),
        grid=(bsz, nc),
        in_specs=_dir_specs(zw, nc, rows) + _dir_specs(LANES, nc, rows) + [full(a) for a in params]
        + [state_spec],
        out_specs=_dir_specs(w, nc, rows) + [state_spec],
        out_shape=[tok, tok, jax.ShapeDtypeStruct(s0.shape, F32)],
        scratch_shapes=[pltpu.VMEM((2, heads // 2, LANES, LANES), F32)],
        compiler_params=_cparams("parallel", "arbitrary"),
    )(z_q, z_q, z_ba, z_ba, *params, s0)


def _s5_disc_kernel(lre_ref, lim_ref, ldt_ref, bre_ref, bim_ref, abr_ref, abi_ref, bbr_ref, bbi_ref):
    lre = lre_ref[...]
    lim = lim_ref[...]
    dt = jnp.exp(ldt_ref[...])
    mag = jnp.exp(lre * dt)
    abr = mag * jnp.cos(lim * dt)
    abi = mag * jnp.sin(lim * dt)
    den = lre * lre + lim * lim
    fr = ((abr - 1.0) * lre + abi * lim) / den
    fi = (abi * lre - (abr - 1.0) * lim) / den
    abr_ref[...] = abr
    abi_ref[...] = abi
    bbr_ref[...] = fr * bre_ref[...] - fi * bim_ref[...]
    bbi_ref[...] = fr * bim_ref[...] + fi * bre_ref[...]


def _s5_discretise(p):
    _, groups, state = p['s5_lam_re'].shape
    ch = p['s5_b_re'].shape[-1]
    rows = 2 * groups * ch
    spread = lambda t: jnp.broadcast_to(t[:, :, None, :], (2, groups, ch, state)).reshape(rows, state)
    lre = spread(p['s5_lam_re'])
    lim = spread(p['s5_lam_im'])
    ldt = spread(jnp.broadcast_to(p['s5_log_dt'][:, :, None], (2, groups, state)))
    bre = jnp.swapaxes(p['s5_b_re'], 2, 3).reshape(rows, state)
    bim = jnp.swapaxes(p['s5_b_im'], 2, 3).reshape(rows, state)
    out = jax.ShapeDtypeStruct((rows, state), F32)
    abr, abi, bbr, bbi = pl.pallas_call(_s5_disc_kernel, out_shape=[out] * 4)(lre, lim, ldt, bre, bim)
    eye = jnp.eye(groups, dtype=F32)
    chain = lambda t: jnp.repeat(t.reshape(2, groups, ch, state)[:, :, 0, :].reshape(2, groups * state),
                                 SUBLANES // 2, axis=0)

    def bmat(t):
        t = t.reshape(2, groups, ch, state)
        return (t[:, :, :, None, :] * eye[None, :, None, :, None]).reshape(2, groups * ch, groups * state)

    def cmat(t):
        t = jnp.swapaxes(t, 2, 3)
        return (t[:, :, :, None, :] * eye[None, :, None, :, None]).reshape(2, groups * state, groups * ch)

    b_all = jnp.concatenate([bmat(bbr), bmat(bbi)], axis=-1).astype(MXU_DTYPE)
    c_all = jnp.stack([cmat(p['s5_c_re']), cmat(p['s5_c_im'])], axis=1).astype(MXU_DTYPE)
    return chain(abr), chain(abi), b_all, c_all


def _s5_kernel(tc, u_ref, b_ref, c_ref, ar_ref, ai_ref, h0r_ref, h0i_ref,
               y_ref, hfr_ref, hfi_ref, bur, bui, hr, hi):
    i = pl.program_id(0)
    rows = tc * SUBLANES
    ns = ar_ref.shape[1]

    @pl.when(i == 0)
    def _():
        hr[...] = h0r_ref[...]
        hi[...] = h0i_ref[...]

    u = u_ref[...].reshape(rows, u_ref.shape[-1])
    fwd_row = (lax.broadcasted_iota(jnp.int32, (rows, 1), 0) % SUBLANES) < (SUBLANES // 2)
    bu = jnp.where(fwd_row, _mm(u, b_ref[0]), _mm(u, b_ref[1]))
    bur[...] = bu[:, :ns]
    bui[...] = bu[:, ns:]
    cw = S5_SCAN_LANES
    for cb in range(ns // cw):
        cols = slice(cb * cw, (cb + 1) * cw)
        ar = ar_ref[:, cols]
        ai = ai_ref[:, cols]

        def body(t, carry, cols=cols, ar=ar, ai=ai):
            h_r, h_i = carry
            o = pl.multiple_of(t * SUBLANES, SUBLANES)
            n_r = ar * h_r - ai * h_i + bur[pl.ds(o, SUBLANES), cols]
            n_i = ar * h_i + ai * h_r + bui[pl.ds(o, SUBLANES), cols]
            bur[pl.ds(o, SUBLANES), cols] = n_r
            bui[pl.ds(o, SUBLANES), cols] = n_i
            return n_r, n_i

        h_r, h_i = lax.fori_loop(0, tc, body, (hr[:, cols], hi[:, cols]), unroll=8)
        hr[:, cols] = h_r
        hi[:, cols] = h_i
    h_re = bur[...]
    h_im = bui[...]
    y = jnp.where(fwd_row,
                  _mm(h_re, c_ref[0, 0]) - _mm(h_im, c_ref[0, 1]),
                  _mm(h_re, c_ref[1, 0]) - _mm(h_im, c_ref[1, 1]))
    y_ref[...] = y.reshape(y_ref.shape)
    hfr_ref[...] = hr[...]
    hfi_ref[...] = hi[...]


def _s5_scan(u2, disc, h0):
    ar, ai, b_all, c_all = disc
    n, chains, w = u2.shape
    ns = ar.shape[1]
    tc = min(S5_BLOCK, n)
    full = lambda a: pl.BlockSpec(a.shape, lambda i, nd=a.ndim: (0,) * nd)
    st = jax.ShapeDtypeStruct((chains, ns), F32)
    y2, hfr, hfi = pl.pallas_call(
        functools.partial(_s5_kernel, tc),
        grid=(n // tc,),
        in_specs=[pl.BlockSpec((tc, chains, w), lambda i: (i, 0, 0)),
                  full(b_all), full(c_all), full(ar), full(ai), full(h0[0]), full(h0[1])],
        out_specs=[pl.BlockSpec((tc, chains, w), lambda i: (i, 0, 0)), full(ar), full(ar)],
        out_shape=[jax.ShapeDtypeStruct((n, chains, w), F32), st, st],
        scratch_shapes=[pltpu.VMEM((tc * chains, ns), F32), pltpu.VMEM((tc * chains, ns), F32),
                        pltpu.VMEM((chains, ns), F32), pltpu.VMEM((chains, ns), F32)],
        compiler_params=_cparams("arbitrary"),
    )(u2, b_all, c_all, ar, ai, h0[0], h0[1])
    return y2, (hfr, hfi)


def _to_scan_order(u, column_major):
    bsz, n, w = u.shape
    if column_major:
        return u.reshape(bsz, n // GRID_W, GRID_W, w).transpose(2, 1, 0, 3).reshape(n, bsz, w)
    return u.transpose(1, 0, 2)


def _from_scan_order(y, column_major):
    n, bsz, w = y.shape
    if column_major:
        return y.reshape(GRID_W, n // GRID_W, bsz, w).transpose(2, 1, 0, 3).reshape(bsz, n, w)
    return y.transpose(1, 0, 2)


def _mix_residual(x_ref, mod_ref, gd_ref, yf_ref, yb_ref, bonf_ref, bonb_ref, of_ref, ob_ref, zg_ref,
                  s5f_ref, s5b_ref, u5_ref, gup_ref, lnxg_ref, lnxb_ref, gdg_ref, s5d_ref, gluw_ref, glub_ref,
                  wout_ref):
    hmean = _head_mean_matrix(yf_ref.shape[-1])
    y = yf_ref[...] + yb_ref[...]
    yc = y - _mm_exact_rhs(y, hmean, STAT_TERMS)
    yn = yc * lax.rsqrt(_mm_exact_rhs(yc * yc, hmean, STAT_TERMS) + LNX_EPS)
    gate = _mm(_sigmoid(gd_ref[...]), gup_ref[...])
    y_rw = (yn * lnxg_ref[...] + lnxb_ref[...] + bonf_ref[...] + bonb_ref[...]) * gate
    o = of_ref[...] + ob_ref[...]
    o = o * lax.rsqrt(_mm_exact_rhs(o * o, hmean, STAT_TERMS) + NORM_EPS) * gdg_ref[...]
    y_gd = o * _silu(zg_ref[...])
    y5 = _gelu_tanh(s5f_ref[...] + s5b_ref[...] + s5d_ref[...] * u5_ref[...])
    y5 = y5 * _sigmoid(_mm(y5, gluw_ref[...]) + glub_ref[...])
    mix = jnp.concatenate([y_rw, y_gd, y5], axis=1)
    return x_ref[...] + mod_ref[2:3, :] * _mm(mix, wout_ref[...])


def _mix_ffn_kernel(hidden, final, n_mix, *refs):
    mix_refs, (g_ref, wgu_ref, wd_ref, fg_ref, o_ref) = refs[:n_mix], refs[n_mix:]
    mod_ref = mix_refs[1]
    x = _mix_residual(*mix_refs)
    ms = jnp.mean(x * x, axis=-1, keepdims=True)
    h = x * lax.rsqrt(ms + NORM_EPS) * g_ref[...]
    h = h * (1.0 + mod_ref[4:5, :]) + mod_ref[3:4, :]
    gu = _mm(h, wgu_ref[...])
    act = _silu(gu[:, :hidden]) * gu[:, hidden:]
    o = x + mod_ref[5:6, :] * _mm(act, wd_ref[...])
    if final:
        ms = jnp.mean(o * o, axis=-1, keepdims=True)
        o = o * lax.rsqrt(ms + NORM_EPS) * fg_ref[...]
    o_ref[...] = o


def _mix_ffn(xs, mod, mod_row, z_rw, toks, params, gain, w_gu, w_down, final_gain, final):
    bsz, n, d = xs.shape
    hidden = w_down.shape[0]
    tm = min(256, n)
    tok_spec = lambda a: pl.BlockSpec((None, tm, a.shape[-1]), lambda b, i: (b, i, 0))
    const = lambda a: pl.BlockSpec(a.shape, lambda b, i, nd=a.ndim: (0,) * nd,
                                   pipeline_mode=pl.Buffered(1))
    gate_in = z_rw.shape[-1] // RW_GATE_LORA - 1
    assert RW_GATE_LORA == LANES and z_rw.shape[-1] % LANES == 0
    mix_specs = ([tok_spec(xs), pl.BlockSpec((None, 6, d), lambda b, i: (mod_row(b), 0, 0)),
                  pl.BlockSpec((None, tm, LANES), lambda b, i: (b, i, gate_in))]
                 + [tok_spec(a) for a in toks] + [const(a) for a in params])
    return pl.pallas_call(
        functools.partial(_mix_ffn_kernel, hidden, final, len(mix_specs)),
        grid=(bsz, n // tm),
        in_specs=mix_specs + [const(gain), const(w_gu), const(w_down), const(final_gain)],
        out_specs=tok_spec(xs),
        out_shape=jax.ShapeDtypeStruct(xs.shape, F32),
        compiler_params=_cparams("parallel", "parallel"),
    )(xs, mod, z_rw, *toks, *params, gain, w_gu, w_down, final_gain)


def kernel(x, c, ctx, c_ctx, norm1_g, norm2_g, final_g, ada_w, ada_b, w_in, w_out,
           rw_mu, rw_w0, rw_wup, rw_a0, rw_aup, rw_gup, rw_kk, rw_ka, rw_rk, rw_lnx_g, rw_lnx_b,
           gd_conv, gd_a_log, gd_dt_bias, gd_norm_g,
           s5_lam_re, s5_lam_im, s5_log_dt, s5_b_re, s5_b_im, s5_c_re, s5_c_im, s5_d, s5_glu_w, s5_glu_b,
           ffn_w_gate, ffn_w_up, ffn_w_down):
    bsz, n, d = x.shape
    depth = w_in.shape[0]
    rw_w = rw_kk.shape[-1]
    gd_w = gd_conv.shape[-1] // 3
    s5_w = s5_d.shape[-1]
    rw_heads = rw_w // HEAD_DIM
    gd_heads = gd_w // HEAD_DIM
    in_rw = rw_mu.shape[-1]
    assert 2 * bsz == SUBLANES and n % GRID_W == 0 and n % CHUNK == 0 and ctx.shape[1] % CHUNK == 0
    assert rw_heads % 2 == 0 and gd_heads % 2 == 0 and 2 * HEAD_DIM == LANES

    cond = jnp.concatenate([c, c_ctx[None, :], jnp.zeros((SUBLANES - bsz - 1, d), F32)], axis=0)
    mod_all = _ada_modulation(cond, ada_w, ada_b).reshape(depth, SUBLANES, 6, d)
    x_row = lambda b: b
    ctx_row = lambda b: bsz

    o_gd = in_rw
    o_ba = o_gd + 3 * gd_w
    o_gate = o_ba + 4 * gd_heads
    o_s5 = o_gate + gd_w
    splits = (in_rw, 3 * gd_w, gd_w, s5_w, LANES)
    w_perm = jnp.concatenate(
        [w_in[:, :, :o_gd], w_in[:, :, o_gd:o_ba], w_in[:, :, o_gate:o_s5], w_in[:, :, o_s5:],
         w_in[:, :, o_ba:o_gate], jnp.zeros((depth, d, LANES - 4 * gd_heads), F32)], axis=-1).astype(MXU_DTYPE)
    w_gu = jnp.concatenate([ffn_w_gate, ffn_w_up], axis=-1).astype(MXU_DTYPE)
    w_dn = ffn_w_down.astype(MXU_DTYPE)
    w_o = w_out.astype(MXU_DTYPE)
    glu_w = s5_glu_w.astype(MXU_DTYPE)
    row = lambda t: t.reshape(1, -1)

    for l in range(depth):
        p = {'rw_mu': row(rw_mu[l]), 'rw_w0': rw_w0[l], 'rw_wup': rw_wup[l], 'rw_a0': rw_a0[l],
             'rw_aup': rw_aup[l], 'rw_gup': rw_gup[l], 'rw_kk': row(rw_kk[l]), 'rw_ka': row(rw_ka[l]),
             'rw_rk': row(rw_rk[l]), 'gd_conv': gd_conv[l], 'gd_a_log': gd_a_log[l],
             'gd_dt_bias': gd_dt_bias[l], 's5_lam_re': s5_lam_re[l], 's5_lam_im': s5_lam_im[l],
             's5_log_dt': s5_log_dt[l], 's5_b_re': s5_b_re[l], 's5_b_im': s5_b_im[l],
             's5_c_re': s5_c_re[l], 's5_c_im': s5_c_im[l]}
        mix_params = [rw_gup[l], row(rw_lnx_g[l]), row(rw_lnx_b[l]), row(jnp.tile(gd_norm_g[l], gd_heads)),
                      row(s5_d[l]), glu_w[l], row(s5_glu_b[l]), w_o[l]]
        disc = _s5_discretise(p)
        mod = mod_all[l]
        states = (jnp.zeros((2, bsz, rw_heads // 2, LANES, LANES), F32),
                  jnp.zeros((2, bsz, gd_heads // 2, LANES, LANES), F32),
                  (jnp.zeros((SUBLANES, disc[0].shape[1]), F32),) * 2)
        new_streams = []
        for stream, mod_row, column_major, is_ctx in ((ctx, ctx_row, False, True), (x, x_row, True, False)):
            z_rw, kk, z_q, z_g, z_s5, z_ba = _in_projection(stream, mod, mod_row, row(norm1_g[l]), w_perm[l], splits,
                                                            p['rw_mu'], p['rw_kk'], p['gd_conv'])
            yf, yb, bonf, bonb, s_rw = _rwkv_scan(z_rw, kk, p, states[0])
            of, ob, s_gd = _gdn_scan(z_q, z_ba, p, states[1])
            ut = _to_scan_order(z_s5, column_major)
            y2, s_s5 = _s5_scan(jnp.concatenate([ut, ut[::-1]], axis=1), disc, states[2])
            s5f = _from_scan_order(y2[:, :bsz], column_major)
            s5b = _from_scan_order(y2[::-1, bsz:], column_major)
            states = (s_rw, s_gd, s_s5)
            if is_ctx and l == depth - 1:
                new_streams.append(stream)
                continue
            final = (not is_ctx) and l == depth - 1
            new_streams.append(_mix_ffn(stream, mod, mod_row, z_rw, [yf, yb, bonf, bonb, of, ob, z_g, s5f, s5b, z_s5],
                                        mix_params, row(norm2_g[l]), w_gu[l], w_dn[l], row(final_g), final))
        ctx, x = new_streams
    return x
```

```python
import functools
import math

import jax
import jax.numpy as jnp
from jax import lax
from jax.experimental import pallas as pl
from jax.experimental.pallas import tpu as pltpu

F32 = jnp.float32
MXU_DTYPE = jnp.bfloat16
HIGHEST = lax.Precision.HIGHEST

NORM_EPS = 1e-6
LNX_EPS = 64e-5
HEAD_DIM = 64
GRID_W = 64
CHUNK = 64
CHUNKS_PER_STEP = 4
STAT_TERMS = 2
SUM_TERMS = 3
SUBLANES = 8
LANES = 128
RW_DECAY_LORA = 64
RW_ICLR_LORA = 64
RW_GATE_LORA = 128
S5_BLOCK = 64
S5_SCAN_LANES = 512
VMEM_LIMIT = 56 * 1024 * 1024


def _cparams(*sem):
    return pltpu.CompilerParams(dimension_semantics=sem, vmem_limit_bytes=VMEM_LIMIT)


def _sigmoid(x):
    return 1.0 / (1.0 + jnp.exp(-x))


def _silu(x):
    return x * _sigmoid(x)


def _softplus(x):
    return jnp.maximum(x, 0.0) + jnp.log(1.0 + jnp.exp(-jnp.abs(x)))


def _gelu_tanh(x):
    return 0.5 * x * (1.0 + jnp.tanh(0.7978845608028654 * (x + 0.044715 * x * x * x)))


def _mm(a, b):
    return jnp.dot(a.astype(MXU_DTYPE), b.astype(MXU_DTYPE), preferred_element_type=F32)


def _mm_nt(a, b):
    return lax.dot_general(a.astype(MXU_DTYPE), b.astype(MXU_DTYPE), (((1,), (1,)), ((), ())),
                           preferred_element_type=F32)


def _mm_tn(a, b):
    return lax.dot_general(a.astype(MXU_DTYPE), b.astype(MXU_DTYPE), (((0,), (0,)), ((), ())),
                           preferred_element_type=F32)


def _mm_hi(a, b):
    return jnp.dot(a, b, precision=HIGHEST, preferred_element_type=F32)


def _split(x, terms):
    parts = []
    for _ in range(terms):
        p = x.astype(MXU_DTYPE)
        parts.append(p)
        x = x - p.astype(F32)
    return parts


def _mm_exact_rhs(a, e, terms):
    e = e.astype(MXU_DTYPE)
    return sum(jnp.dot(p, e, preferred_element_type=F32) for p in _split(a, terms))


def _mm_exact_lhs(e, b, terms):
    e = e.astype(MXU_DTYPE)
    return sum(jnp.dot(e, p, preferred_element_type=F32) for p in _split(b, terms))


def _mm_nt_exact_lhs(e, b, terms):
    e = e.astype(MXU_DTYPE)
    return sum(lax.dot_general(e, p, (((1,), (1,)), ((), ())), preferred_element_type=F32)
               for p in _split(b, terms))


def _head_mean_matrix(width):
    r = lax.broadcasted_iota(jnp.int32, (width, width), 0) // HEAD_DIM
    c = lax.broadcasted_iota(jnp.int32, (width, width), 1) // HEAD_DIM
    return jnp.where(r == c, 1.0 / HEAD_DIM, 0.0).astype(F32)


def _tri_masks(n, reverse):
    t = lax.broadcasted_iota(jnp.int32, (n, n), 0)
    j = lax.broadcasted_iota(jnp.int32, (n, n), 1)
    if reverse:
        return j >= t, j > t, t == j
    return j <= t, j < t, t == j


def _shifted(z, prev_row, next_row):
    n = z.shape[0]
    row = lax.broadcasted_iota(jnp.int32, z.shape, 0)
    zp = jnp.where(row == 0, prev_row, pltpu.roll(z, 1, axis=0))
    zn = jnp.where(row == n - 1, next_row, pltpu.roll(z, n - 1, axis=0))
    return zp, zn


def _block_tri_masks(n, reverse):
    t = lax.broadcasted_iota(jnp.int32, (n, n), 0)
    j = lax.broadcasted_iota(jnp.int32, (n, n), 1)
    order = (j >= t) if reverse else (j <= t)
    return order & (t // CHUNK == j // CHUNK)


def _pair_index(shape):
    t = lax.broadcasted_iota(jnp.int32, shape, 0)
    j = lax.broadcasted_iota(jnp.int32, shape, 1) % HEAD_DIM
    return t, j


def _pair_tri_masks(reverse):
    t, j = _pair_index((CHUNK, 2 * CHUNK))
    if reverse:
        return j >= t, j > t
    return j <= t, j < t


def _bd(x):
    x = x.astype(MXU_DTYPE)
    lo = lax.broadcasted_iota(jnp.int32, x.shape, 1) < HEAD_DIM
    zero = jnp.zeros_like(x)
    return jnp.concatenate([jnp.where(lo, x, zero), jnp.where(lo, zero, x)], axis=0)


def _bd_mask(n):
    r = lax.broadcasted_iota(jnp.int32, (n, n), 0) // HEAD_DIM
    c = lax.broadcasted_iota(jnp.int32, (n, n), 1) // HEAD_DIM
    return r == c


def _unit_triangular_inverses(nmats):
    c = nmats[0].shape[0]
    t, j = _pair_index(nmats[0].shape)
    eye = (t == j).astype(F32)
    s = SUBLANES
    same = t // s == j // s
    ps = [jnp.where(same, m, 0.0) for m in nmats]
    xs = [eye + p for p in ps]
    for _ in range(2):
        ps = [_mm(p, _bd(p)) for p in ps]
        xs = [x + _mm(p, _bd(x)) for p, x in zip(ps, xs)]
    while s < c:
        sel = (t // (2 * s) == j // (2 * s)) & (t // s != j // s)
        cx = [_mm(jnp.where(sel, m, 0.0), _bd(x)) for m, x in zip(nmats, xs)]
        xs = [x + _mm(x, _bd(y)) for x, y in zip(xs, cx)]
        s *= 2
    return xs


def _scan_state_free_stage(items):
    c, n = items[0]['v'].shape
    keep = _bd_mask(n)
    xs = _unit_triangular_inverses([it['a_ab'] for it in items])
    av = [_mm(jnp.concatenate([it['a_ak'], it['a_rk']], axis=0), _bd(it['v'])) for it in items]
    ws = [_mm(x, jnp.concatenate([_bd(it['at']), _bd(a[:c])], axis=1)) for x, it, a in zip(xs, items, av)]
    kv = [jnp.where(keep, _mm_tn(it['v'], it['kg']), 0.0) for it in items]
    for it, w, a, m in zip(items, ws, av, kv):
        it['lhs'] = jnp.concatenate([w[:, :n], it['rt']], axis=0)
        it['p2'] = w[:, n:]
        it['y0'] = a[c:]
        it['kv'] = m


def _scan_state_stage(items, states):
    c, n = items[0]['p2'].shape
    keep = _bd_mask(n)
    ps = [_mm_nt(it['lhs'], s) for it, s in zip(items, states)]
    us = [p[:c] + it['p2'] for p, it in zip(ps, items)]
    ys = [p[c:] + _mm(it['a_rb'], _bd(u)) + it['y0'] for p, it, u in zip(ps, items, us)]
    new = [s * it['dec'] + jnp.where(keep, _mm_tn(u, it['bg']), 0.0) + it['kv']
           for s, it, u in zip(states, items, us)]
    return ys, new


def _run_chains(items, pairs, nb, s_scr):
    _scan_state_free_stage(list(items.values()))
    chains = [(d, p) for d in range(2) for p in range(pairs)]
    states = [s_scr[d, p] for d, p in chains]
    ys = {}
    for step in range(nb):
        keys = [(d, p, nb - 1 - step if d else step) for d, p in chains]
        out, states = _scan_state_stage([items[k] for k in keys], states)
        ys.update(zip(keys, out))
    for (d, p), s in zip(chains, states):
        s_scr[d, p] = s
    return [jnp.concatenate([jnp.concatenate([ys[(d, p, j)] for p in range(pairs)], axis=1)
                             for j in range(nb)], axis=0) for d in range(2)]


def _dplr_items(d, heads, nb, r, lw, k, v, a, b):
    n, w = r.shape
    incl, strict = _pair_tri_masks(d == 1)
    g = _mm_exact_lhs(_block_tri_masks(n, d == 1), lw, SUM_TERMS)
    gts = [jnp.sum(lw[j * CHUNK:(j + 1) * CHUNK], axis=0, keepdims=True) for j in range(nb)]
    gtb = jnp.concatenate([jnp.broadcast_to(gt, (CHUNK, w)) for gt in gts], axis=0)
    e_neg = jnp.exp(-g)
    e_rem = jnp.exp(gtb - g)
    rt = r * jnp.exp(g)
    at = a * jnp.exp(g - lw)
    bt = b * e_neg
    kt = k * e_neg
    bg = b * e_rem
    kg = k * e_rem
    pw = 2 * CHUNK
    items = {}
    for p in range(heads // 2):
        for j in range(nb):
            rows = slice(j * CHUNK, (j + 1) * CHUNK)
            cols = slice(p * LANES, (p + 1) * LANES)
            am = _mm_nt(jnp.concatenate([at[rows, cols], rt[rows, cols]], axis=0),
                        jnp.concatenate([_bd(bt[rows, cols]), _bd(kt[rows, cols])], axis=0))
            items[(d, p, j)] = dict(
                at=at[rows, cols], rt=rt[rows, cols], v=v[rows, cols], bg=bg[rows, cols], kg=kg[rows, cols],
                dec=jnp.exp(gts[j][:, cols]),
                a_ab=jnp.where(strict, am[:CHUNK, :pw], 0.0), a_ak=jnp.where(strict, am[:CHUNK, pw:], 0.0),
                a_rb=jnp.where(incl, am[CHUNK:, :pw], 0.0), a_rk=jnp.where(incl, am[CHUNK:, pw:], 0.0))
    return items


def _gdn_items(d, heads, nb, q, k, v, beta_all, g_all, lg_all, rows_t):
    n, w = q.shape
    assert CHUNK == HEAD_DIM
    incl, strict = _pair_tri_masks(d == 1)

    def dec(col, row, mask):
        return jnp.where(mask, jnp.exp(jnp.minimum(col - row, 0.0)), 0.0)

    src = lax.broadcasted_iota(jnp.int32, (LANES, w), 0)
    head = lax.broadcasted_iota(jnp.int32, (LANES, w), 1) // HEAD_DIM + d * heads
    beta = _mm_exact_rhs(beta_all, src == head, STAT_TERMS)
    gl = _mm_exact_rhs(jnp.concatenate([g_all, lg_all], axis=0), src == head + 2 * heads, SUM_TERMS)
    g, lg = gl[:n], gl[n:]
    gx = g - lg
    gts = [jnp.sum(lg[j * CHUNK:(j + 1) * CHUNK], axis=0, keepdims=True) for j in range(nb)]
    gtb = jnp.concatenate([jnp.broadcast_to(gt, (CHUNK, w)) for gt in gts], axis=0)
    bk = k * beta
    at = k * jnp.exp(gx)
    rt = q * jnp.exp(g)
    bg = -bk * jnp.exp(gtb - gx)
    kg = bk * jnp.exp(gtb - g)
    items = {}
    for p in range(heads // 2):
        cg = (2 + d) * heads + 2 * p
        for j in range(nb):
            rows = slice(j * CHUNK, (j + 1) * CHUNK)
            cols = slice(p * LANES, (p + 1) * LANES)
            g_col, gx_col = g[rows, cols], gx[rows, cols]
            pair_row = lambda o: jnp.concatenate([rows_t[cg:cg + 1, o + j * CHUNK:o + (j + 1) * CHUNK],
                                                  rows_t[cg + 1:cg + 2, o + j * CHUNK:o + (j + 1) * CHUNK]], axis=1)
            g_row = pair_row(0)
            gx_row = g_row - pair_row(n)
            gram = _mm_nt(jnp.concatenate([k[rows, cols], q[rows, cols]], axis=0), _bd(bk[rows, cols]))
            kbk, qbk = gram[:CHUNK], gram[CHUNK:]
            items[(d, p, j)] = dict(
                at=at[rows, cols], rt=rt[rows, cols], v=v[rows, cols], bg=bg[rows, cols], kg=kg[rows, cols],
                dec=jnp.exp(gts[j][:, cols]),
                a_ab=-kbk * dec(gx_col, gx_row, strict), a_ak=kbk * dec(gx_col, g_row, strict),
                a_rb=-qbk * dec(g_col, gx_row, incl), a_rk=qbk * dec(g_col, g_row, incl))
    return items


def _ada_kernel(cond_ref, w_ref, b_ref, o_ref):
    cnd = cond_ref[...]
    o_ref[...] = _mm_hi(_silu(cnd), w_ref[...]) + b_ref[...]


def _ada_modulation(cond, ada_w, ada_b):
    depth, d, n6 = ada_w.shape
    tn = n6 // 4
    return pl.pallas_call(
        _ada_kernel,
        grid=(depth, n6 // tn),
        in_specs=[pl.BlockSpec((SUBLANES, d), lambda l, j: (0, 0)),
                  pl.BlockSpec((None, d, tn), lambda l, j: (l, 0, j)),
                  pl.BlockSpec((None, 1, tn), lambda l, j: (l, 0, j))],
        out_specs=pl.BlockSpec((None, SUBLANES, tn), lambda l, j: (l, 0, j)),
        out_shape=jax.ShapeDtypeStruct((depth, SUBLANES, n6), F32),
        compiler_params=_cparams("parallel", "parallel"),
    )(cond, ada_w, ada_b.reshape(depth, 1, n6))


def _in_kernel(nt, splits, x_ref, xp_ref, xn_ref, mod_ref, g_ref, w_ref, mu_ref, kk_ref, conv_ref,
               rw_ref, kkn_ref, q_ref, g_out_ref, s5_ref, ba_ref):
    i = pl.program_id(1)
    in_rw, gd3, gd_w, s5_w, _ = splits
    rw_w = kk_ref.shape[-1]

    def norm_mod(x):
        ms = jnp.mean(x * x, axis=-1, keepdims=True)
        h = x * lax.rsqrt(ms + NORM_EPS) * g_ref[...]
        return h * (1.0 + mod_ref[1:2, :]) + mod_ref[0:1, :]

    z = _mm(norm_mod(x_ref[...]), w_ref[...])
    ns = in_rw + gd3
    zh = _mm(norm_mod(jnp.concatenate([xp_ref[...], xn_ref[...]], axis=0)), w_ref[:, :ns])
    prev_row = jnp.where(i > 0, zh[SUBLANES - 1:SUBLANES], 0.0)
    next_row = jnp.where(i < nt - 1, zh[SUBLANES:SUBLANES + 1], 0.0)
    zs = z[:, :ns]
    zp, zn = _shifted(zs, prev_row, next_row)

    zr = zs[:, :in_rw]
    zr = zr + (0.5 * (zp[:, :in_rw] + zn[:, :in_rw]) - zr) * mu_ref[...]
    hmean = _head_mean_matrix(rw_w)
    kk = zr[:, rw_w:2 * rw_w] * kk_ref[...]
    kkn_ref[...] = kk * lax.rsqrt(_mm_exact_rhs(kk * kk, hmean, STAT_TERMS) * HEAD_DIM + NORM_EPS)
    rw_ref[:, :3 * rw_w] = zr[:, :3 * rw_w]
    lora = zr[:, 3 * rw_w:3 * rw_w + LANES]
    is_decay = lax.broadcasted_iota(jnp.int32, lora.shape, 1) < RW_DECAY_LORA
    rw_ref[:, 3 * rw_w:3 * rw_w + LANES] = jnp.where(is_decay, jnp.tanh(lora), lora)
    rw_ref[:, 3 * rw_w + LANES:] = zr[:, 3 * rw_w + LANES:]

    sq = slice(in_rw, ns)
    xc = _silu(conv_ref[0:1, :] * zp[:, sq] + conv_ref[1:2, :] * zs[:, sq] + conv_ref[2:3, :] * zn[:, sq])
    q = xc[:, :gd_w]
    k = xc[:, gd_w:2 * gd_w]
    hmean = _head_mean_matrix(gd_w)
    q_ref[:, :gd_w] = q * lax.rsqrt(_mm_exact_rhs(q * q, hmean, STAT_TERMS) * HEAD_DIM + NORM_EPS) * (HEAD_DIM ** -0.5)
    q_ref[:, gd_w:2 * gd_w] = k * lax.rsqrt(_mm_exact_rhs(k * k, hmean, STAT_TERMS) * HEAD_DIM + NORM_EPS)
    q_ref[:, 2 * gd_w:] = xc[:, 2 * gd_w:]

    off = ns
    for o_ref, width in zip((g_out_ref, s5_ref, ba_ref), splits[2:]):
        o_ref[...] = z[:, off:off + width]
        off += width


def _in_projection(xs, mod, mod_row, gain, w_perm, splits, mu, kk, conv):
    bsz, n, d = xs.shape
    tm = min(512, n)
    nt = n // tm
    tiles = tm // SUBLANES
    total = sum(splits)
    widths = (splits[0], kk.shape[-1]) + tuple(splits[1:])
    const = lambda a: pl.BlockSpec(a.shape, lambda b, i, nd=a.ndim: (0,) * nd)
    return pl.pallas_call(
        functools.partial(_in_kernel, nt, splits),
        grid=(bsz, nt),
        in_specs=[pl.BlockSpec((None, tm, d), lambda b, i: (b, i, 0)),
                  pl.BlockSpec((None, SUBLANES, d), lambda b, i: (b, jnp.maximum(i * tiles - 1, 0), 0)),
                  pl.BlockSpec((None, SUBLANES, d), lambda b, i: (b, jnp.minimum((i + 1) * tiles, nt * tiles - 1), 0)),
                  pl.BlockSpec((None, 6, d), lambda b, i: (mod_row(b), 0, 0)),
                  pl.BlockSpec((1, d), lambda b, i: (0, 0)),
                  pl.BlockSpec((d, total), lambda b, i: (0, 0)),
                  const(mu), const(kk), const(conv)],
        out_specs=[pl.BlockSpec((None, tm, w), lambda b, i: (b, i, 0)) for w in widths],
        out_shape=[jax.ShapeDtypeStruct((bsz, n, w), F32) for w in widths],
        compiler_params=_cparams("parallel", "parallel"),
    )(xs, xs, xs, mod, gain, w_perm, mu, kk, conv)


def _block_rows(n):
    nb = CHUNKS_PER_STEP
    while n % (nb * CHUNK):
        nb //= 2
    return nb * CHUNK


def _dir_specs(width, nc, rows):
    return [pl.BlockSpec((None, rows, width), lambda b, i: (b, i, 0)),
            pl.BlockSpec((None, rows, width), lambda b, i: (b, nc - 1 - i, 0))]


def _rw_kernel(nc, heads,
               zf_ref, zb_ref, kkf_ref, kkb_ref,
               w0_ref, wup_ref, a0_ref, aup_ref, ka_ref, rk_ref, s0_ref,
               yf_ref, yb_ref, bonf_ref, bonb_ref, sfin_ref, s_scr):
    i = pl.program_id(1)
    w = heads * HEAD_DIM

    @pl.when(i == 0)
    def _():
        s_scr[...] = s0_ref[...]

    hmean = _head_mean_matrix(w)
    nb = zf_ref.shape[0] // CHUNK
    items = {}
    views = ((zf_ref, kkf_ref, bonf_ref), (zb_ref, kkb_ref, bonb_ref))
    for d, (z_ref, kk_ref, bon_ref) in enumerate(views):
        r = z_ref[:, :w]
        k = z_ref[:, w:2 * w]
        v = z_ref[:, 2 * w:3 * w]
        o = 3 * w
        wd_t = z_ref[:, o:o + RW_DECAY_LORA]
        ad = z_ref[:, o + RW_DECAY_LORA:o + RW_DECAY_LORA + RW_ICLR_LORA]
        kk = kk_ref[...]
        lw = -math.exp(-0.5) * _sigmoid(w0_ref[d:d + 1, :] + _mm(wd_t, wup_ref[d]))
        a = _sigmoid(a0_ref[d:d + 1, :] + _mm(ad, aup_ref[d]))
        kd = k * (1.0 + (a - 1.0) * ka_ref[...])
        bon_ref[...] = _mm_exact_rhs(r * kd * rk_ref[...], hmean, STAT_TERMS) * HEAD_DIM * v
        items.update(_dplr_items(d, heads, nb, r, lw, kd, v, -kk, kk * a))
    yf_ref[...], yb_ref[...] = _run_chains(items, heads // 2, nb, s_scr)

    @pl.when(i == nc - 1)
    def _():
        sfin_ref[...] = s_scr[...]


def _rwkv_scan(z_rw, kk, p, s0):
    bsz, n, zw = z_rw.shape
    rows = _block_rows(n)
    nc = n // rows
    w = kk.shape[-1]
    heads = w // HEAD_DIM
    full = lambda a: pl.BlockSpec(a.shape, lambda b, i, nd=a.ndim: (0,) * nd)
    params = [p['rw_w0'], p['rw_wup'], p['rw_a0'], p['rw_aup'], p['rw_ka'], p['rw_rk']]
    state_spec = pl.BlockSpec((2, None, heads // 2, LANES, LANES), lambda b, i: (0, b, 0, 0, 0))
    fwd, bwd = _dir_specs(w, nc, rows)
    tok = jax.ShapeDtypeStruct((bsz, n, w), F32)
    return pl.pallas_call(
        functools.partial(_rw_kernel, nc, heads),
        grid=(bsz, nc),
        in_specs=_dir_specs(zw, nc, rows) + [fwd, bwd] + [full(a) for a in params] + [state_spec],
        out_specs=[fwd, bwd, fwd, bwd, state_spec],
        out_shape=[tok, tok, tok, tok, jax.ShapeDtypeStruct(s0.shape, F32)],
        scratch_shapes=[pltpu.VMEM((2, heads // 2, LANES, LANES), F32)],
        compiler_params=_cparams("parallel", "arbitrary"),
    )(z_rw, z_rw, kk, kk, *params, s0)


def _gd_kernel(nc, heads,
               zf_ref, zb_ref, baf_ref, bab_ref, alog_ref, dtb_ref, s0_ref,
               of_ref, ob_ref, sfin_ref, s_scr):
    i = pl.program_id(1)
    w = heads * HEAD_DIM

    @pl.when(i == 0)
    def _():
        s_scr[...] = s0_ref[...]

    lane_eye = _tri_masks(LANES, False)[2].astype(F32)
    nb = zf_ref.shape[0] // CHUNK
    items = {}
    for d, (z_ref, ba_ref) in enumerate(((zf_ref, baf_ref), (zb_ref, bab_ref))):
        q = z_ref[:, :w]
        k = z_ref[:, w:2 * w]
        v = z_ref[:, 2 * w:]
        ba = ba_ref[...]
        beta_all = _sigmoid(ba)
        lg_all = -jnp.exp(alog_ref[...]) * _softplus(ba + dtb_ref[...])
        g_all = _mm_exact_lhs(_block_tri_masks(ba.shape[0], d == 1), lg_all, SUM_TERMS)
        rows_t = _mm_nt_exact_lhs(lane_eye, jnp.concatenate([g_all, lg_all], axis=0), SUM_TERMS)
        items.update(_gdn_items(d, heads, nb, q, k, v, beta_all, g_all, lg_all, rows_t))
    of_ref[...], ob_ref[...] = _run_chains(items, heads // 2, nb, s_scr)

    @pl.when(i == nc - 1)
    def _():
        sfin_ref[...] = s_scr[...]


def _gdn_scan(z_q, z_ba, p, s0):
    bsz, n, zw = z_q.shape
    rows = _block_rows(n)
    nc = n // rows
    w = zw // 3
    heads = w // HEAD_DIM
    full = lambda a: pl.BlockSpec(a.shape, lambda b, i, nd=a.ndim: (0,) * nd)
    pad = lambda t: jnp.pad(t.reshape(1, 2 * heads), ((0, 0), (2 * heads, LANES - 4 * heads)))
    params = [pad(p['gd_a_log']), pad(p['gd_dt_bias'])]
    state_spec = pl.BlockSpec((2, None, heads // 2, LANES, LANES), lambda b, i: (0, b, 0, 0, 0))
    tok = jax.ShapeDtypeStruct((bsz, n, w), F32)
    return pl.pallas_call(
        functools.partial(_gd_kernel, nc, heads),
        grid=(bsz, nc),
        in_specs=_dir_specs(zw, nc, rows) + _dir_specs(LANES, nc, rows) + [full(a) for a in params]
        + [state_spec],
        out_specs=_dir_specs(w, nc, rows) + [state_spec],
        out_shape=[tok, tok, jax.ShapeDtypeStruct(s0.shape, F32)],
        scratch_shapes=[pltpu.VMEM((2, heads // 2, LANES, LANES), F32)],
        compiler_params=_cparams("parallel", "arbitrary"),
    )(z_q, z_q, z_ba, z_ba, *params, s0)


def _s5_disc_kernel(lre_ref, lim_ref, ldt_ref, bre_ref, bim_ref, abr_ref, abi_ref, bbr_ref, bbi_ref):
    lre = lre_ref[...]
    lim = lim_ref[...]
    dt = jnp.exp(ldt_ref[...])
    mag = jnp.exp(lre * dt)
    abr = mag * jnp.cos(lim * dt)
    abi = mag * jnp.sin(lim * dt)
    den = lre * lre + lim * lim
    fr = ((abr - 1.0) * lre + abi * lim) / den
    fi = (abi * lre - (abr - 1.0) * lim) / den
    abr_ref[...] = abr
    abi_ref[...] = abi
    bbr_ref[...] = fr * bre_ref[...] - fi * bim_ref[...]
    bbi_ref[...] = fr * bim_ref[...] + fi * bre_ref[...]


def _s5_discretise(p):
    _, groups, state = p['s5_lam_re'].shape
    ch = p['s5_b_re'].shape[-1]
    rows = 2 * groups * ch
    spread = lambda t: jnp.broadcast_to(t[:, :, None, :], (2, groups, ch, state)).reshape(rows, state)
    lre = spread(p['s5_lam_re'])
    lim = spread(p['s5_lam_im'])
    ldt = spread(jnp.broadcast_to(p['s5_log_dt'][:, :, None], (2, groups, state)))
    bre = jnp.swapaxes(p['s5_b_re'], 2, 3).reshape(rows, state)
    bim = jnp.swapaxes(p['s5_b_im'], 2, 3).reshape(rows, state)
    out = jax.ShapeDtypeStruct((rows, state), F32)
    abr, abi, bbr, bbi = pl.pallas_call(_s5_disc_kernel, out_shape=[out] * 4)(lre, lim, ldt, bre, bim)
    eye = jnp.eye(groups, dtype=F32)
    chain = lambda t: jnp.repeat(t.reshape(2, groups, ch, state)[:, :, 0, :].reshape(2, groups * state),
                                 SUBLANES // 2, axis=0)

    def bmat(t):
        t = t.reshape(2, groups, ch, state)
        return (t[:, :, :, None, :] * eye[None, :, None, :, None]).reshape(2, groups * ch, groups * state)

    def cmat(t):
        t = jnp.swapaxes(t, 2, 3)
        return (t[:, :, :, None, :] * eye[None, :, None, :, None]).reshape(2, groups * state, groups * ch)

    b_all = jnp.concatenate([bmat(bbr), bmat(bbi)], axis=-1).astype(MXU_DTYPE)
    c_all = jnp.stack([cmat(p['s5_c_re']), cmat(p['s5_c_im'])], axis=1).astype(MXU_DTYPE)
    return chain(abr), chain(abi), b_all, c_all


def _s5_kernel(tc, uf_ref, ub_ref, b_ref, c_ref, ar_ref, ai_ref, h0r_ref, h0i_ref,
               yf_ref, yb_ref, hfr_ref, hfi_ref, u_scr, bur, bui, hr, hi):
    i = pl.program_id(0)
    rows = tc * SUBLANES
    half = SUBLANES // 2
    ns = ar_ref.shape[1]

    @pl.when(i == 0)
    def _():
        hr[...] = h0r_ref[...]
        hi[...] = h0i_ref[...]

    u_scr[:, :half, :] = uf_ref[...]
    for t in range(tc):
        u_scr[t, half:, :] = ub_ref[tc - 1 - t]
    u = u_scr[...].reshape(rows, u_scr.shape[-1])
    fwd_row = (lax.broadcasted_iota(jnp.int32, (rows, 1), 0) % SUBLANES) < (SUBLANES // 2)
    bu = jnp.where(fwd_row, _mm(u, b_ref[0]), _mm(u, b_ref[1]))
    bur[...] = bu[:, :ns]
    bui[...] = bu[:, ns:]
    cw = S5_SCAN_LANES
    for cb in range(ns // cw):
        cols = slice(cb * cw, (cb + 1) * cw)
        ar = ar_ref[:, cols]
        ai = ai_ref[:, cols]

        def body(t, carry, cols=cols, ar=ar, ai=ai):
            h_r, h_i = carry
            o = pl.multiple_of(t * SUBLANES, SUBLANES)
            n_r = ar * h_r - ai * h_i + bur[pl.ds(o, SUBLANES), cols]
            n_i = ar * h_i + ai * h_r + bui[pl.ds(o, SUBLANES), cols]
            bur[pl.ds(o, SUBLANES), cols] = n_r
            bui[pl.ds(o, SUBLANES), cols] = n_i
            return n_r, n_i

        h_r, h_i = lax.fori_loop(0, tc, body, (hr[:, cols], hi[:, cols]), unroll=8)
        hr[:, cols] = h_r
        hi[:, cols] = h_i
    h_re = bur[...]
    h_im = bui[...]
    y = jnp.where(fwd_row,
                  _mm(h_re, c_ref[0, 0]) - _mm(h_im, c_ref[0, 1]),
                  _mm(h_re, c_ref[1, 0]) - _mm(h_im, c_ref[1, 1]))
    y3 = y.reshape(tc, SUBLANES, y.shape[-1])
    yf_ref[...] = y3[:, :half, :]
    for t in range(tc):
        yb_ref[tc - 1 - t] = y3[t, half:, :]
    hfr_ref[...] = hr[...]
    hfi_ref[...] = hi[...]


def _s5_scan(ut, disc, h0):
    ar, ai, b_all, c_all = disc
    n, bsz, w = ut.shape
    chains = 2 * bsz
    assert chains == SUBLANES
    ns = ar.shape[1]
    tc = min(S5_BLOCK, n)
    nblk = n // tc
    full = lambda a: pl.BlockSpec(a.shape, lambda i, nd=a.ndim: (0,) * nd)
    fwd = pl.BlockSpec((tc, bsz, w), lambda i: (i, 0, 0))
    bwd = pl.BlockSpec((tc, bsz, w), lambda i: (nblk - 1 - i, 0, 0))
    st = jax.ShapeDtypeStruct((chains, ns), F32)
    tok = jax.ShapeDtypeStruct(ut.shape, F32)
    yf, yb, hfr, hfi = pl.pallas_call(
        functools.partial(_s5_kernel, tc),
        grid=(nblk,),
        in_specs=[fwd, bwd, full(b_all), full(c_all), full(ar), full(ai), full(h0[0]), full(h0[1])],
        out_specs=[fwd, bwd, full(ar), full(ar)],
        out_shape=[tok, tok, st, st],
        scratch_shapes=[pltpu.VMEM((tc, chains, w), F32),
                        pltpu.VMEM((tc * chains, ns), F32), pltpu.VMEM((tc * chains, ns), F32),
                        pltpu.VMEM((chains, ns), F32), pltpu.VMEM((chains, ns), F32)],
        compiler_params=_cparams("arbitrary"),
    )(ut, ut, b_all, c_all, ar, ai, h0[0], h0[1])
    return yf, yb, (hfr, hfi)


def _to_scan_order(u, column_major):
    bsz, n, w = u.shape
    if column_major:
        return u.reshape(bsz, n // GRID_W, GRID_W, w).transpose(2, 1, 0, 3).reshape(n, bsz, w)
    return u.transpose(1, 0, 2)


def _from_scan_order(y, column_major):
    n, bsz, w = y.shape
    if column_major:
        return y.reshape(GRID_W, n // GRID_W, bsz, w).transpose(2, 1, 0, 3).reshape(bsz, n, w)
    return y.transpose(1, 0, 2)


def _mix_residual(x_ref, mod_ref, gd_ref, yf_ref, yb_ref, bonf_ref, bonb_ref, of_ref, ob_ref, zg_ref,
                  s5f_ref, s5b_ref, u5_ref, gup_ref, lnxg_ref, lnxb_ref, gdg_ref, s5d_ref, gluw_ref, glub_ref,
                  wout_ref):
    hmean = _head_mean_matrix(yf_ref.shape[-1])
    y = yf_ref[...] + yb_ref[...]
    yc = y - _mm_exact_rhs(y, hmean, STAT_TERMS)
    yn = yc * lax.rsqrt(_mm_exact_rhs(yc * yc, hmean, STAT_TERMS) + LNX_EPS)
    gate = _mm(_sigmoid(gd_ref[...]), gup_ref[...])
    y_rw = (yn * lnxg_ref[...] + lnxb_ref[...] + bonf_ref[...] + bonb_ref[...]) * gate
    o = of_ref[...] + ob_ref[...]
    o = o * lax.rsqrt(_mm_exact_rhs(o * o, hmean, STAT_TERMS) + NORM_EPS) * gdg_ref[...]
    y_gd = o * _silu(zg_ref[...])
    y5 = _gelu_tanh(s5f_ref[...] + s5b_ref[...] + s5d_ref[...] * u5_ref[...])
    y5 = y5 * _sigmoid(_mm(y5, gluw_ref[...]) + glub_ref[...])
    mix = jnp.concatenate([y_rw, y_gd, y5], axis=1)
    return x_ref[...] + mod_ref[2:3, :] * _mm(mix, wout_ref[...])


def _mix_ffn_kernel(hidden, final, n_mix, *refs):
    mix_refs, (g_ref, wgu_ref, wd_ref, fg_ref, o_ref) = refs[:n_mix], refs[n_mix:]
    mod_ref = mix_refs[1]
    x = _mix_residual(*mix_refs)
    ms = jnp.mean(x * x, axis=-1, keepdims=True)
    h = x * lax.rsqrt(ms + NORM_EPS) * g_ref[...]
    h = h * (1.0 + mod_ref[4:5, :]) + mod_ref[3:4, :]
    gu = _mm(h, wgu_ref[...])
    act = _silu(gu[:, :hidden]) * gu[:, hidden:]
    o = x + mod_ref[5:6, :] * _mm(act, wd_ref[...])
    if final:
        ms = jnp.mean(o * o, axis=-1, keepdims=True)
        o = o * lax.rsqrt(ms + NORM_EPS) * fg_ref[...]
    o_ref[...] = o


def _mix_ffn(xs, mod, mod_row, z_rw, toks, params, gain, w_gu, w_down, final_gain, final):
    bsz, n, d = xs.shape
    hidden = w_down.shape[0]
    tm = min(512, n)
    tok_spec = lambda a: pl.BlockSpec((None, tm, a.shape[-1]), lambda b, i: (b, i, 0))
    const = lambda a: pl.BlockSpec(a.shape, lambda b, i, nd=a.ndim: (0,) * nd,
                                   pipeline_mode=pl.Buffered(1))
    gate_in = z_rw.shape[-1] // RW_GATE_LORA - 1
    assert RW_GATE_LORA == LANES and z_rw.shape[-1] % LANES == 0
    mix_specs = ([tok_spec(xs), pl.BlockSpec((None, 6, d), lambda b, i: (mod_row(b), 0, 0)),
                  pl.BlockSpec((None, tm, LANES), lambda b, i: (b, i, gate_in))]
                 + [tok_spec(a) for a in toks] + [const(a) for a in params])
    return pl.pallas_call(
        functools.partial(_mix_ffn_kernel, hidden, final, len(mix_specs)),
        grid=(bsz, n // tm),
        in_specs=mix_specs + [const(gain), const(w_gu), const(w_down), const(final_gain)],
        out_specs=tok_spec(xs),
        out_shape=jax.ShapeDtypeStruct(xs.shape, F32),
        compiler_params=_cparams("parallel", "parallel"),
    )(xs, mod, z_rw, *toks, *params, gain, w_gu, w_down, final_gain)


def kernel(x, c, ctx, c_ctx, norm1_g, norm2_g, final_g, ada_w, ada_b, w_in, w_out,
           rw_mu, rw_w0, rw_wup, rw_a0, rw_aup, rw_gup, rw_kk, rw_ka, rw_rk, rw_lnx_g, rw_lnx_b,
           gd_conv, gd_a_log, gd_dt_bias, gd_norm_g,
           s5_lam_re, s5_lam_im, s5_log_dt, s5_b_re, s5_b_im, s5_c_re, s5_c_im, s5_d, s5_glu_w, s5_glu_b,
           ffn_w_gate, ffn_w_up, ffn_w_down):
    bsz, n, d = x.shape
    depth = w_in.shape[0]
    rw_w = rw_kk.shape[-1]
    gd_w = gd_conv.shape[-1] // 3
    s5_w = s5_d.shape[-1]
    rw_heads = rw_w // HEAD_DIM
    gd_heads = gd_w // HEAD_DIM
    in_rw = rw_mu.shape[-1]
    assert 2 * bsz == SUBLANES and n % GRID_W == 0 and n % CHUNK == 0 and ctx.shape[1] % CHUNK == 0
    assert rw_heads % 2 == 0 and gd_heads % 2 == 0 and 2 * HEAD_DIM == LANES

    cond = jnp.concatenate([c, c_ctx[None, :], jnp.zeros((SUBLANES - bsz - 1, d), F32)], axis=0)
    mod_all = _ada_modulation(cond, ada_w, ada_b).reshape(depth, SUBLANES, 6, d)
    x_row = lambda b: b
    ctx_row = lambda b: bsz

    o_gd = in_rw
    o_ba = o_gd + 3 * gd_w
    o_gate = o_ba + 4 * gd_heads
    o_s5 = o_gate + gd_w
    splits = (in_rw, 3 * gd_w, gd_w, s5_w, LANES)
    w_perm = jnp.concatenate(
        [w_in[:, :, :o_gd], w_in[:, :, o_gd:o_ba], w_in[:, :, o_gate:o_s5], w_in[:, :, o_s5:],
         w_in[:, :, o_ba:o_gate], jnp.zeros((depth, d, LANES - 4 * gd_heads), F32)], axis=-1).astype(MXU_DTYPE)
    w_gu = jnp.concatenate([ffn_w_gate, ffn_w_up], axis=-1).astype(MXU_DTYPE)
    w_dn = ffn_w_down.astype(MXU_DTYPE)
    w_o = w_out.astype(MXU_DTYPE)
    glu_w = s5_glu_w.astype(MXU_DTYPE)
    row = lambda t: t.reshape(1, -1)

    for l in range(depth):
        p = {'rw_mu': row(rw_mu[l]), 'rw_w0': rw_w0[l], 'rw_wup': rw_wup[l], 'rw_a0': rw_a0[l],
             'rw_aup': rw_aup[l], 'rw_gup': rw_gup[l], 'rw_kk': row(rw_kk[l]), 'rw_ka': row(rw_ka[l]),
             'rw_rk': row(rw_rk[l]), 'gd_conv': gd_conv[l], 'gd_a_log': gd_a_log[l],
             'gd_dt_bias': gd_dt_bias[l], 's5_lam_re': s5_lam_re[l], 's5_lam_im': s5_lam_im[l],
             's5_log_dt': s5_log_dt[l], 's5_b_re': s5_b_re[l], 's5_b_im': s5_b_im[l],
             's5_c_re': s5_c_re[l], 's5_c_im': s5_c_im[l]}
        mix_params = [rw_gup[l], row(rw_lnx_g[l]), row(rw_lnx_b[l]), row(jnp.tile(gd_norm_g[l], gd_heads)),
                      row(s5_d[l]), glu_w[l], row(s5_glu_b[l]), w_o[l]]
        disc = _s5_discretise(p)
        mod = mod_all[l]
        states = (jnp.zeros((2, bsz, rw_heads // 2, LANES, LANES), F32),
                  jnp.zeros((2, bsz, gd_heads // 2, LANES, LANES), F32),
                  (jnp.zeros((SUBLANES, disc[0].shape[1]), F32),) * 2)
        new_streams = []
        for stream, mod_row, column_major, is_ctx in ((ctx, ctx_row, False, True), (x, x_row, True, False)):
            z_rw, kk, z_q, z_g, z_s5, z_ba = _in_projection(stream, mod, mod_row, row(norm1_g[l]), w_perm[l], splits,
                                                            p['rw_mu'], p['rw_kk'], p['gd_conv'])
            yf, yb, bonf, bonb, s_rw = _rwkv_scan(z_rw, kk, p, states[0])
            of, ob, s_gd = _gdn_scan(z_q, z_ba, p, states[1])
            yf5, yb5, s_s5 = _s5_scan(_to_scan_order(z_s5, column_major), disc, states[2])
            s5f = _from_scan_order(yf5, column_major)
            s5b = _from_scan_order(yb5, column_major)
            states = (s_rw, s_gd, s_s5)
            if is_ctx and l == depth - 1:
                new_streams.append(stream)
                continue
            final = (not is_ctx) and l == depth - 1
            new_streams.append(_mix_ffn(stream, mod, mod_row, z_rw, [yf, yb, bonf, bonb, of, ob, z_g, s5f, s5b, z_s5],
                                        mix_params, row(norm2_g[l]), w_gu[l], w_dn[l], row(final_g), final))
        ctx, x = new_streams
    return x
```

```python
import functools
import math

import jax
import jax.numpy as jnp
from jax import lax
from jax.experimental import pallas as pl
from jax.experimental.pallas import tpu as pltpu

F32 = jnp.float32
MXU_DTYPE = jnp.bfloat16
HIGHEST = lax.Precision.HIGHEST

NORM_EPS = 1e-6
LNX_EPS = 64e-5
HEAD_DIM = 64
GRID_W = 64
CHUNK = 64
CHUNKS_PER_STEP = 4
STAT_TERMS = 2
SUM_TERMS = 3
SUBLANES = 8
LANES = 128
RW_DECAY_LORA = 64
RW_ICLR_LORA = 64
RW_GATE_LORA = 128
ROW_TILE = 512
S5_BLOCK = 128
S5_SCAN_LANES = 512
VMEM_LIMIT = 56 * 1024 * 1024


def _cparams(*sem):
    return pltpu.CompilerParams(dimension_semantics=sem, vmem_limit_bytes=VMEM_LIMIT)


def _sigmoid(x):
    return 0.5 + 0.5 * jnp.tanh(0.5 * x)


def _silu(x):
    return x * _sigmoid(x)


def _softplus(x):
    return jnp.maximum(x, 0.0) + jnp.log(1.0 + jnp.exp(-jnp.abs(x)))


def _gelu_tanh(x):
    return 0.5 * x * (1.0 + jnp.tanh(0.7978845608028654 * (x + 0.044715 * x * x * x)))


def _mm(a, b):
    return jnp.dot(a.astype(MXU_DTYPE), b.astype(MXU_DTYPE), preferred_element_type=F32)


def _mm_nt(a, b):
    return lax.dot_general(a.astype(MXU_DTYPE), b.astype(MXU_DTYPE), (((1,), (1,)), ((), ())),
                           preferred_element_type=F32)


def _mm_tn(a, b):
    return lax.dot_general(a.astype(MXU_DTYPE), b.astype(MXU_DTYPE), (((0,), (0,)), ((), ())),
                           preferred_element_type=F32)


def _mm_hi(a, b):
    return jnp.dot(a, b, precision=HIGHEST, preferred_element_type=F32)


def _split(x, terms):
    parts = []
    for _ in range(terms):
        p = x.astype(MXU_DTYPE)
        parts.append(p)
        x = x - p.astype(F32)
    return parts


def _mm_exact_rhs(a, e, terms):
    e = e.astype(MXU_DTYPE)
    return sum(jnp.dot(p, e, preferred_element_type=F32) for p in _split(a, terms))


def _mm_exact_lhs(e, b, terms):
    e = e.astype(MXU_DTYPE)
    return sum(jnp.dot(e, p, preferred_element_type=F32) for p in _split(b, terms))


def _mm_nt_exact_lhs(e, b, terms):
    e = e.astype(MXU_DTYPE)
    return sum(lax.dot_general(e, p, (((1,), (1,)), ((), ())), preferred_element_type=F32)
               for p in _split(b, terms))


def _head_mean_matrix(width):
    r = lax.broadcasted_iota(jnp.int32, (width, width), 0) // HEAD_DIM
    c = lax.broadcasted_iota(jnp.int32, (width, width), 1) // HEAD_DIM
    return jnp.where(r == c, 1.0 / HEAD_DIM, 0.0).astype(F32)


def _tri_masks(n, reverse):
    t = lax.broadcasted_iota(jnp.int32, (n, n), 0)
    j = lax.broadcasted_iota(jnp.int32, (n, n), 1)
    if reverse:
        return j >= t, j > t, t == j
    return j <= t, j < t, t == j


def _shifted(z, prev_row, next_row):
    n = z.shape[0]
    row = lax.broadcasted_iota(jnp.int32, z.shape, 0)
    zp = jnp.where(row == 0, prev_row, pltpu.roll(z, 1, axis=0))
    zn = jnp.where(row == n - 1, next_row, pltpu.roll(z, n - 1, axis=0))
    return zp, zn


def _block_tri_masks(n, reverse):
    t = lax.broadcasted_iota(jnp.int32, (n, n), 0)
    j = lax.broadcasted_iota(jnp.int32, (n, n), 1)
    order = (j >= t) if reverse else (j <= t)
    return order & (t // CHUNK == j // CHUNK)


def _pair_index(shape):
    t = lax.broadcasted_iota(jnp.int32, shape, 0)
    j = lax.broadcasted_iota(jnp.int32, shape, 1) % HEAD_DIM
    return t, j


def _pair_tri_masks(reverse):
    t, j = _pair_index((CHUNK, 2 * CHUNK))
    if reverse:
        return j >= t, j > t
    return j <= t, j < t


def _bd(x):
    x = x.astype(MXU_DTYPE)
    lo = lax.broadcasted_iota(jnp.int32, x.shape, 1) < HEAD_DIM
    zero = jnp.zeros_like(x)
    return jnp.concatenate([jnp.where(lo, x, zero), jnp.where(lo, zero, x)], axis=0)


def _bd_mask(n):
    r = lax.broadcasted_iota(jnp.int32, (n, n), 0) // HEAD_DIM
    c = lax.broadcasted_iota(jnp.int32, (n, n), 1) // HEAD_DIM
    return r == c


def _unit_triangular_inverses(nmats):
    c = nmats[0].shape[0]
    t, j = _pair_index(nmats[0].shape)
    eye = (t == j).astype(F32)
    s = SUBLANES
    same = t // s == j // s
    ps = [jnp.where(same, m, 0.0) for m in nmats]
    xs = [eye + p for p in ps]
    for _ in range(2):
        ps = [_mm(p, _bd(p)) for p in ps]
        xs = [x + _mm(p, _bd(x)) for p, x in zip(ps, xs)]
    while s < c:
        sel = (t // (2 * s) == j // (2 * s)) & (t // s != j // s)
        cx = [_mm(jnp.where(sel, m, 0.0), _bd(x)) for m, x in zip(nmats, xs)]
        xs = [x + _mm(x, _bd(y)) for x, y in zip(xs, cx)]
        s *= 2
    return xs


def _scan_state_free_stage(items):
    c, n = items[0]['v'].shape
    keep = _bd_mask(n)
    xs = _unit_triangular_inverses([it['a_ab'] for it in items])
    av = [_mm(jnp.concatenate([it['a_ak'], it['a_rk']], axis=0), _bd(it['v'])) for it in items]
    ws = [_mm(x, jnp.concatenate([_bd(it['at']), _bd(a[:c])], axis=1)) for x, it, a in zip(xs, items, av)]
    kv = [jnp.where(keep, _mm_tn(it['v'], it['kg']), 0.0) for it in items]
    for it, w, a, m in zip(items, ws, av, kv):
        it['lhs'] = jnp.concatenate([w[:, :n], it['rt']], axis=0)
        it['p2'] = w[:, n:]
        it['y0'] = a[c:]
        it['kv'] = m


def _scan_state_stage(items, states):
    c, n = items[0]['p2'].shape
    keep = _bd_mask(n)
    ps = [_mm_nt(it['lhs'], s) for it, s in zip(items, states)]
    us = [p[:c] + it['p2'] for p, it in zip(ps, items)]
    ys = [p[c:] + _mm(it['a_rb'], _bd(u)) + it['y0'] for p, it, u in zip(ps, items, us)]
    new = [s * it['dec'] + jnp.where(keep, _mm_tn(u, it['bg']), 0.0) + it['kv']
           for s, it, u in zip(states, items, us)]
    return ys, new


def _run_chains(items, pairs, nb, s_scr):
    _scan_state_free_stage(list(items.values()))
    chains = [(d, p) for d in range(2) for p in range(pairs)]
    states = [s_scr[d, p] for d, p in chains]
    ys = {}
    for step in range(nb):
        keys = [(d, p, nb - 1 - step if d else step) for d, p in chains]
        out, states = _scan_state_stage([items[k] for k in keys], states)
        ys.update(zip(keys, out))
    for (d, p), s in zip(chains, states):
        s_scr[d, p] = s
    return [jnp.concatenate([jnp.concatenate([ys[(d, p, j)] for p in range(pairs)], axis=1)
                             for j in range(nb)], axis=0) for d in range(2)]


def _dplr_items(d, heads, nb, r, lw, k, v, a, b):
    n, w = r.shape
    incl, strict = _pair_tri_masks(d == 1)
    g = _mm_exact_lhs(_block_tri_masks(n, d == 1), lw, SUM_TERMS)
    gts = [jnp.sum(lw[j * CHUNK:(j + 1) * CHUNK], axis=0, keepdims=True) for j in range(nb)]
    gtb = jnp.concatenate([jnp.broadcast_to(gt, (CHUNK, w)) for gt in gts], axis=0)
    e_neg = jnp.exp(-g)
    e_rem = jnp.exp(gtb - g)
    rt = r * jnp.exp(g)
    at = a * jnp.exp(g - lw)
    bt = b * e_neg
    kt = k * e_neg
    bg = b * e_rem
    kg = k * e_rem
    pw = 2 * CHUNK
    items = {}
    for p in range(heads // 2):
        for j in range(nb):
            rows = slice(j * CHUNK, (j + 1) * CHUNK)
            cols = slice(p * LANES, (p + 1) * LANES)
            am = _mm_nt(jnp.concatenate([at[rows, cols], rt[rows, cols]], axis=0),
                        jnp.concatenate([_bd(bt[rows, cols]), _bd(kt[rows, cols])], axis=0))
            items[(d, p, j)] = dict(
                at=at[rows, cols], rt=rt[rows, cols], v=v[rows, cols], bg=bg[rows, cols], kg=kg[rows, cols],
                dec=jnp.exp(gts[j][:, cols]),
                a_ab=jnp.where(strict, am[:CHUNK, :pw], 0.0), a_ak=jnp.where(strict, am[:CHUNK, pw:], 0.0),
                a_rb=jnp.where(incl, am[CHUNK:, :pw], 0.0), a_rk=jnp.where(incl, am[CHUNK:, pw:], 0.0))
    return items


def _gdn_items(d, heads, nb, q, k, v, beta_all, g_all, lg_all, rows_t):
    n, w = q.shape
    assert CHUNK == HEAD_DIM
    incl, strict = _pair_tri_masks(d == 1)

    def dec(col, row, mask):
        return jnp.where(mask, jnp.exp(jnp.minimum(col - row, 0.0)), 0.0)

    src = lax.broadcasted_iota(jnp.int32, (LANES, w), 0)
    head = lax.broadcasted_iota(jnp.int32, (LANES, w), 1) // HEAD_DIM + d * heads
    beta = _mm_exact_rhs(beta_all, src == head, STAT_TERMS)
    gl = _mm_exact_rhs(jnp.concatenate([g_all, lg_all], axis=0), src == head + 2 * heads, SUM_TERMS)
    g, lg = gl[:n], gl[n:]
    gx = g - lg
    gts = [jnp.sum(lg[j * CHUNK:(j + 1) * CHUNK], axis=0, keepdims=True) for j in range(nb)]
    gtb = jnp.concatenate([jnp.broadcast_to(gt, (CHUNK, w)) for gt in gts], axis=0)
    bk = k * beta
    at = k * jnp.exp(gx)
    rt = q * jnp.exp(g)
    bg = -bk * jnp.exp(gtb - gx)
    kg = bk * jnp.exp(gtb - g)
    items = {}
    for p in range(heads // 2):
        cg = (2 + d) * heads + 2 * p
        for j in range(nb):
            rows = slice(j * CHUNK, (j + 1) * CHUNK)
            cols = slice(p * LANES, (p + 1) * LANES)
            g_col, gx_col = g[rows, cols], gx[rows, cols]
            pair_row = lambda o: jnp.concatenate([rows_t[cg:cg + 1, o + j * CHUNK:o + (j + 1) * CHUNK],
                                                  rows_t[cg + 1:cg + 2, o + j * CHUNK:o + (j + 1) * CHUNK]], axis=1)
            g_row = pair_row(0)
            e_row = jnp.exp(pair_row(n))
            d_ak = dec(gx_col, g_row, strict)
            d_rk = dec(g_col, g_row, incl)
            gram = _mm_nt(jnp.concatenate([k[rows, cols], q[rows, cols]], axis=0), _bd(bk[rows, cols]))
            kbk, qbk = gram[:CHUNK], gram[CHUNK:]
            items[(d, p, j)] = dict(
                at=at[rows, cols], rt=rt[rows, cols], v=v[rows, cols], bg=bg[rows, cols], kg=kg[rows, cols],
                dec=jnp.exp(gts[j][:, cols]),
                a_ab=-kbk * (d_ak * e_row), a_ak=kbk * d_ak,
                a_rb=-qbk * (d_rk * e_row), a_rk=qbk * d_rk)
    return items


def _ada_kernel(cond_ref, w_ref, b_ref, o_ref):
    cnd = cond_ref[...]
    o_ref[...] = _mm_hi(_silu(cnd), w_ref[...]) + b_ref[...]


def _ada_modulation(cond, ada_w, ada_b):
    depth, d, n6 = ada_w.shape
    tn = n6 // 4
    return pl.pallas_call(
        _ada_kernel,
        grid=(depth, n6 // tn),
        in_specs=[pl.BlockSpec((SUBLANES, d), lambda l, j: (0, 0)),
                  pl.BlockSpec((None, d, tn), lambda l, j: (l, 0, j)),
                  pl.BlockSpec((None, 1, tn), lambda l, j: (l, 0, j))],
        out_specs=pl.BlockSpec((None, SUBLANES, tn), lambda l, j: (l, 0, j)),
        out_shape=jax.ShapeDtypeStruct((depth, SUBLANES, n6), F32),
        compiler_params=_cparams("parallel", "parallel"),
    )(cond, ada_w, ada_b.reshape(depth, 1, n6))


def _in_kernel(nt, splits, x_ref, xp_ref, xn_ref, mod_ref, g_ref, w_ref, mu_ref, kk_ref, conv_ref,
               rw_ref, kkn_ref, q_ref, g_out_ref, s5_ref, ba_ref):
    i = pl.program_id(1)
    in_rw, gd3, gd_w, s5_w, _ = splits
    rw_w = kk_ref.shape[-1]

    def norm_mod(x):
        ms = jnp.mean(x * x, axis=-1, keepdims=True)
        h = x * lax.rsqrt(ms + NORM_EPS) * g_ref[...]
        return h * (1.0 + mod_ref[1:2, :]) + mod_ref[0:1, :]

    z = _mm(norm_mod(x_ref[...]), w_ref[...])
    ns = in_rw + gd3
    zh = _mm(norm_mod(jnp.concatenate([xp_ref[...], xn_ref[...]], axis=0)), w_ref[:, :ns])
    prev_row = jnp.where(i > 0, zh[SUBLANES - 1:SUBLANES], 0.0)
    next_row = jnp.where(i < nt - 1, zh[SUBLANES:SUBLANES + 1], 0.0)
    zs = z[:, :ns]
    zp, zn = _shifted(zs, prev_row, next_row)

    zr = zs[:, :in_rw]
    zr = zr + (0.5 * (zp[:, :in_rw] + zn[:, :in_rw]) - zr) * mu_ref[...]
    hmean = _head_mean_matrix(rw_w)
    kk = zr[:, rw_w:2 * rw_w] * kk_ref[...]
    kkn_ref[...] = kk * lax.rsqrt(_mm_exact_rhs(kk * kk, hmean, STAT_TERMS) * HEAD_DIM + NORM_EPS)
    rw_ref[:, :3 * rw_w] = zr[:, :3 * rw_w]
    lora = zr[:, 3 * rw_w:3 * rw_w + LANES]
    is_decay = lax.broadcasted_iota(jnp.int32, lora.shape, 1) < RW_DECAY_LORA
    rw_ref[:, 3 * rw_w:3 * rw_w + LANES] = jnp.where(is_decay, jnp.tanh(lora), lora)
    rw_ref[:, 3 * rw_w + LANES:] = zr[:, 3 * rw_w + LANES:]

    sq = slice(in_rw, ns)
    xc = _silu(conv_ref[0:1, :] * zp[:, sq] + conv_ref[1:2, :] * zs[:, sq] + conv_ref[2:3, :] * zn[:, sq])
    q = xc[:, :gd_w]
    k = xc[:, gd_w:2 * gd_w]
    hmean = _head_mean_matrix(gd_w)
    q_ref[:, :gd_w] = q * lax.rsqrt(_mm_exact_rhs(q * q, hmean, STAT_TERMS) * HEAD_DIM + NORM_EPS) * (HEAD_DIM ** -0.5)
    q_ref[:, gd_w:2 * gd_w] = k * lax.rsqrt(_mm_exact_rhs(k * k, hmean, STAT_TERMS) * HEAD_DIM + NORM_EPS)
    q_ref[:, 2 * gd_w:] = xc[:, 2 * gd_w:]

    off = ns
    for o_ref, width in zip((g_out_ref, s5_ref, ba_ref), splits[2:]):
        o_ref[...] = z[:, off:off + width]
        off += width


def _in_projection(xs, mod, mod_row, gain, w_perm, splits, mu, kk, conv):
    bsz, n, d = xs.shape
    tm = min(ROW_TILE, n)
    nt = n // tm
    tiles = tm // SUBLANES
    total = sum(splits)
    widths = (splits[0], kk.shape[-1]) + tuple(splits[1:])
    const = lambda a: pl.BlockSpec(a.shape, lambda b, i, nd=a.ndim: (0,) * nd)
    return pl.pallas_call(
        functools.partial(_in_kernel, nt, splits),
        grid=(bsz, nt),
        in_specs=[pl.BlockSpec((None, tm, d), lambda b, i: (b, i, 0)),
                  pl.BlockSpec((None, SUBLANES, d), lambda b, i: (b, jnp.maximum(i * tiles - 1, 0), 0)),
                  pl.BlockSpec((None, SUBLANES, d), lambda b, i: (b, jnp.minimum((i + 1) * tiles, nt * tiles - 1), 0)),
                  pl.BlockSpec((None, 6, d), lambda b, i: (mod_row(b), 0, 0)),
                  pl.BlockSpec((1, d), lambda b, i: (0, 0)),
                  pl.BlockSpec((d, total), lambda b, i: (0, 0)),
                  const(mu), const(kk), const(conv)],
        out_specs=[pl.BlockSpec((None, tm, w), lambda b, i: (b, i, 0)) for w in widths],
        out_shape=[jax.ShapeDtypeStruct((bsz, n, w), F32) for w in widths],
        compiler_params=_cparams("parallel", "parallel"),
    )(xs, xs, xs, mod, gain, w_perm, mu, kk, conv)


def _block_rows(n):
    nb = CHUNKS_PER_STEP
    while n % (nb * CHUNK):
        nb //= 2
    return nb * CHUNK


def _dir_specs(width, nc, rows):
    return [pl.BlockSpec((None, rows, width), lambda b, i: (b, i, 0)),
            pl.BlockSpec((None, rows, width), lambda b, i: (b, nc - 1 - i, 0))]


def _rw_kernel(nc, heads,
               zf_ref, zb_ref, kkf_ref, kkb_ref,
               w0_ref, wup_ref, a0_ref, aup_ref, ka_ref, rk_ref, s0_ref,
               yf_ref, yb_ref, bonf_ref, bonb_ref, sfin_ref, s_scr):
    i = pl.program_id(1)
    w = heads * HEAD_DIM

    @pl.when(i == 0)
    def _():
        s_scr[...] = s0_ref[...]

    hmean = _head_mean_matrix(w)
    nb = zf_ref.shape[0] // CHUNK
    items = {}
    views = ((zf_ref, kkf_ref, bonf_ref), (zb_ref, kkb_ref, bonb_ref))
    for d, (z_ref, kk_ref, bon_ref) in enumerate(views):
        r = z_ref[:, :w]
        k = z_ref[:, w:2 * w]
        v = z_ref[:, 2 * w:3 * w]
        o = 3 * w
        wd_t = z_ref[:, o:o + RW_DECAY_LORA]
        ad = z_ref[:, o + RW_DECAY_LORA:o + RW_DECAY_LORA + RW_ICLR_LORA]
        kk = kk_ref[...]
        lw = -math.exp(-0.5) * _sigmoid(w0_ref[d:d + 1, :] + _mm(wd_t, wup_ref[d]))
        a = _sigmoid(a0_ref[d:d + 1, :] + _mm(ad, aup_ref[d]))
        kd = k * (1.0 + (a - 1.0) * ka_ref[...])
        bon_ref[...] = _mm_exact_rhs(r * kd * rk_ref[...], hmean, STAT_TERMS) * HEAD_DIM * v
        items.update(_dplr_items(d, heads, nb, r, lw, kd, v, -kk, kk * a))
    yf_ref[...], yb_ref[...] = _run_chains(items, heads // 2, nb, s_scr)

    @pl.when(i == nc - 1)
    def _():
        sfin_ref[...] = s_scr[...]


def _rwkv_scan(z_rw, kk, p, s0):
    bsz, n, zw = z_rw.shape
    rows = _block_rows(n)
    nc = n // rows
    w = kk.shape[-1]
    heads = w // HEAD_DIM
    full = lambda a: pl.BlockSpec(a.shape, lambda b, i, nd=a.ndim: (0,) * nd)
    params = [p['rw_w0'], p['rw_wup'], p['rw_a0'], p['rw_aup'], p['rw_ka'], p['rw_rk']]
    state_spec = pl.BlockSpec((2, None, heads // 2, LANES, LANES), lambda b, i: (0, b, 0, 0, 0))
    fwd, bwd = _dir_specs(w, nc, rows)
    tok = jax.ShapeDtypeStruct((bsz, n, w), F32)
    return pl.pallas_call(
        functools.partial(_rw_kernel, nc, heads),
        grid=(bsz, nc),
        in_specs=_dir_specs(zw, nc, rows) + [fwd, bwd] + [full(a) for a in params] + [state_spec],
        out_specs=[fwd, bwd, fwd, bwd, state_spec],
        out_shape=[tok, tok, tok, tok, jax.ShapeDtypeStruct(s0.shape, F32)],
        scratch_shapes=[pltpu.VMEM((2, heads // 2, LANES, LANES), F32)],
        compiler_params=_cparams("parallel", "arbitrary"),
    )(z_rw, z_rw, kk, kk, *params, s0)


def _gd_kernel(nc, heads,
               zf_ref, zb_ref, baf_ref, bab_ref, alog_ref, dtb_ref, s0_ref,
               of_ref, ob_ref, sfin_ref, s_scr):
    i = pl.program_id(1)
    w = heads * HEAD_DIM

    @pl.when(i == 0)
    def _():
        s_scr[...] = s0_ref[...]

    lane_eye = _tri_masks(LANES, False)[2].astype(F32)
    nb = zf_ref.shape[0] // CHUNK
    items = {}
    for d, (z_ref, ba_ref) in enumerate(((zf_ref, baf_ref), (zb_ref, bab_ref))):
        q = z_ref[:, :w]
        k = z_ref[:, w:2 * w]
        v = z_ref[:, 2 * w:]
        ba = ba_ref[...]
        beta_all = _sigmoid(ba)
        lg_all = -jnp.exp(alog_ref[...]) * _softplus(ba + dtb_ref[...])
        g_all = _mm_exact_lhs(_block_tri_masks(ba.shape[0], d == 1), lg_all, SUM_TERMS)
        rows_t = _mm_nt_exact_lhs(lane_eye, jnp.concatenate([g_all, lg_all], axis=0), SUM_TERMS)
        items.update(_gdn_items(d, heads, nb, q, k, v, beta_all, g_all, lg_all, rows_t))
    of_ref[...], ob_ref[...] = _run_chains(items, heads // 2, nb, s_scr)

    @pl.when(i == nc - 1)
    def _():
        sfin_ref[...] = s_scr[...]


def _gdn_scan(z_q, z_ba, p, s0):
    bsz, n, zw = z_q.shape
    rows = _block_rows(n)
    nc = n // rows
    w = zw // 3
    heads = w // HEAD_DIM
    full = lambda a: pl.BlockSpec(a.shape, lambda b, i, nd=a.ndim: (0,) * nd)
    pad = lambda t: jnp.pad(t.reshape(1, 2 * heads), ((0, 0), (2 * heads, LANES - 4 * heads)))
    params = [pad(p['gd_a_log']), pad(p['gd_dt_bias'])]
    state_spec = pl.BlockSpec((2, None, heads // 2, LANES, LANES), lambda b, i: (0, b, 0, 0, 0))
    tok = jax.ShapeDtypeStruct((bsz, n, w), F32)
    return pl.pallas_call(
        functools.partial(_gd_kernel, nc, heads),
        grid=(bsz, nc),
        in_specs=_dir_specs(zw, nc, rows) + _dir_specs(LANES, nc, rows) + [full(a) for a in params]
        + [state_spec],
        out_specs=_dir_specs(w, nc, rows) + [state_spec],
        out_shape=[tok, tok, jax.ShapeDtypeStruct(s0.shape, F32)],
        scratch_shapes=[pltpu.VMEM((2, heads // 2, LANES, LANES), F32)],
        compiler_params=_cparams("parallel", "arbitrary"),
    )(z_q, z_q, z_ba, z_ba, *params, s0)


def _s5_disc_kernel(lre_ref, lim_ref, ldt_ref, bre_ref, bim_ref, abr_ref, abi_ref, bbr_ref, bbi_ref):
    lre = lre_ref[...]
    lim = lim_ref[...]
    dt = jnp.exp(ldt_ref[...])
    mag = jnp.exp(lre * dt)
    abr = mag * jnp.cos(lim * dt)
    abi = mag * jnp.sin(lim * dt)
    den = lre * lre + lim * lim
    fr = ((abr - 1.0) * lre + abi * lim) / den
    fi = (abi * lre - (abr - 1.0) * lim) / den
    abr_ref[...] = abr
    abi_ref[...] = abi
    bbr_ref[...] = fr * bre_ref[...] - fi * bim_ref[...]
    bbi_ref[...] = fr * bim_ref[...] + fi * bre_ref[...]


def _s5_discretise(p):
    _, groups, state = p['s5_lam_re'].shape
    ch = p['s5_b_re'].shape[-1]
    rows = 2 * groups * ch
    spread = lambda t: jnp.broadcast_to(t[:, :, None, :], (2, groups, ch, state)).reshape(rows, state)
    lre = spread(p['s5_lam_re'])
    lim = spread(p['s5_lam_im'])
    ldt = spread(jnp.broadcast_to(p['s5_log_dt'][:, :, None], (2, groups, state)))
    bre = jnp.swapaxes(p['s5_b_re'], 2, 3).reshape(rows, state)
    bim = jnp.swapaxes(p['s5_b_im'], 2, 3).reshape(rows, state)
    out = jax.ShapeDtypeStruct((rows, state), F32)
    abr, abi, bbr, bbi = pl.pallas_call(_s5_disc_kernel, out_shape=[out] * 4)(lre, lim, ldt, bre, bim)
    eye = jnp.eye(groups, dtype=F32)
    chain = lambda t: jnp.repeat(t.reshape(2, groups, ch, state)[:, :, 0, :].reshape(2, groups * state),
                                 SUBLANES // 2, axis=0)

    def bmat(t):
        t = t.reshape(2, groups, ch, state)
        return (t[:, :, :, None, :] * eye[None, :, None, :, None]).reshape(2, groups * ch, groups * state)

    def cmat(t):
        t = jnp.swapaxes(t, 2, 3)
        return (t[:, :, :, None, :] * eye[None, :, None, :, None]).reshape(2, groups * state, groups * ch)

    b_all = jnp.concatenate([bmat(bbr), bmat(bbi)], axis=-1).astype(MXU_DTYPE)
    c_all = jnp.stack([cmat(p['s5_c_re']), cmat(p['s5_c_im'])], axis=1).astype(MXU_DTYPE)
    return chain(abr), chain(abi), b_all, c_all


def _s5_kernel(tc, uf_ref, ub_ref, b_ref, c_ref, ar_ref, ai_ref, h0r_ref, h0i_ref,
               yf_ref, yb_ref, hfr_ref, hfi_ref, u_scr, bur, bui, hr, hi):
    i = pl.program_id(0)
    rows = tc * SUBLANES
    half = SUBLANES // 2
    ns = ar_ref.shape[1]

    @pl.when(i == 0)
    def _():
        hr[...] = h0r_ref[...]
        hi[...] = h0i_ref[...]

    u_scr[:, :half, :] = uf_ref[...]
    for t in range(tc):
        u_scr[t, half:, :] = ub_ref[tc - 1 - t]
    u = u_scr[...].reshape(rows, u_scr.shape[-1])
    fwd_row = (lax.broadcasted_iota(jnp.int32, (rows, 1), 0) % SUBLANES) < (SUBLANES // 2)
    bu = jnp.where(fwd_row, _mm(u, b_ref[0]), _mm(u, b_ref[1]))
    bur[...] = bu[:, :ns]
    bui[...] = bu[:, ns:]
    cw = S5_SCAN_LANES
    for cb in range(ns // cw):
        cols = slice(cb * cw, (cb + 1) * cw)
        ar = ar_ref[:, cols]
        ai = ai_ref[:, cols]

        def body(t, carry, cols=cols, ar=ar, ai=ai):
            h_r, h_i = carry
            o = pl.multiple_of(t * SUBLANES, SUBLANES)
            n_r = ar * h_r - ai * h_i + bur[pl.ds(o, SUBLANES), cols]
            n_i = ar * h_i + ai * h_r + bui[pl.ds(o, SUBLANES), cols]
            bur[pl.ds(o, SUBLANES), cols] = n_r
            bui[pl.ds(o, SUBLANES), cols] = n_i
            return n_r, n_i

        h_r, h_i = lax.fori_loop(0, tc, body, (hr[:, cols], hi[:, cols]), unroll=8)
        hr[:, cols] = h_r
        hi[:, cols] = h_i
    h_re = bur[...]
    h_im = bui[...]
    y = jnp.where(fwd_row,
                  _mm(h_re, c_ref[0, 0]) - _mm(h_im, c_ref[0, 1]),
                  _mm(h_re, c_ref[1, 0]) - _mm(h_im, c_ref[1, 1]))
    y3 = y.reshape(tc, SUBLANES, y.shape[-1])
    yf_ref[...] = y3[:, :half, :]
    for t in range(tc):
        yb_ref[tc - 1 - t] = y3[t, half:, :]
    hfr_ref[...] = hr[...]
    hfi_ref[...] = hi[...]


def _s5_scan(ut, disc, h0):
    ar, ai, b_all, c_all = disc
    n, bsz, w = ut.shape
    chains = 2 * bsz
    assert chains == SUBLANES
    ns = ar.shape[1]
    tc = min(S5_BLOCK, n)
    nblk = n // tc
    full = lambda a: pl.BlockSpec(a.shape, lambda i, nd=a.ndim: (0,) * nd)
    fwd = pl.BlockSpec((tc, bsz, w), lambda i: (i, 0, 0))
    bwd = pl.BlockSpec((tc, bsz, w), lambda i: (nblk - 1 - i, 0, 0))
    st = jax.ShapeDtypeStruct((chains, ns), F32)
    tok = jax.ShapeDtypeStruct(ut.shape, F32)
    yf, yb, hfr, hfi = pl.pallas_call(
        functools.partial(_s5_kernel, tc),
        grid=(nblk,),
        in_specs=[fwd, bwd, full(b_all), full(c_all), full(ar), full(ai), full(h0[0]), full(h0[1])],
        out_specs=[fwd, bwd, full(ar), full(ar)],
        out_shape=[tok, tok, st, st],
        scratch_shapes=[pltpu.VMEM((tc, chains, w), F32),
                        pltpu.VMEM((tc * chains, ns), F32), pltpu.VMEM((tc * chains, ns), F32),
                        pltpu.VMEM((chains, ns), F32), pltpu.VMEM((chains, ns), F32)],
        compiler_params=_cparams("arbitrary"),
    )(ut, ut, b_all, c_all, ar, ai, h0[0], h0[1])
    return yf, yb, (hfr, hfi)


def _to_scan_order(u, column_major):
    bsz, n, w = u.shape
    if column_major:
        return u.reshape(bsz, n // GRID_W, GRID_W, w).transpose(2, 1, 0, 3).reshape(n, bsz, w)
    return u.transpose(1, 0, 2)


def _from_scan_order(y, column_major):
    n, bsz, w = y.shape
    if column_major:
        return y.reshape(GRID_W, n // GRID_W, bsz, w).transpose(2, 1, 0, 3).reshape(bsz, n, w)
    return y.transpose(1, 0, 2)


def _mix_residual(x_ref, mod_ref, gd_ref, yf_ref, yb_ref, bonf_ref, bonb_ref, of_ref, ob_ref, zg_ref,
                  s5f_ref, s5b_ref, u5_ref, gup_ref, lnxg_ref, lnxb_ref, gdg_ref, s5d_ref, gluw_ref, glub_ref,
                  wout_ref):
    hmean = _head_mean_matrix(yf_ref.shape[-1])
    y = yf_ref[...] + yb_ref[...]
    yc = y - _mm_exact_rhs(y, hmean, STAT_TERMS)
    yn = yc * lax.rsqrt(_mm_exact_rhs(yc * yc, hmean, STAT_TERMS) + LNX_EPS)
    gate = _mm(_sigmoid(gd_ref[...]), gup_ref[...])
    y_rw = (yn * lnxg_ref[...] + lnxb_ref[...] + bonf_ref[...] + bonb_ref[...]) * gate
    o = of_ref[...] + ob_ref[...]
    o = o * lax.rsqrt(_mm_exact_rhs(o * o, hmean, STAT_TERMS) + NORM_EPS) * gdg_ref[...]
    y_gd = o * _silu(zg_ref[...])
    y5 = _gelu_tanh(s5f_ref[...] + s5b_ref[...] + s5d_ref[...] * u5_ref[...])
    y5 = y5 * _sigmoid(_mm(y5, gluw_ref[...]) + glub_ref[...])
    mix = jnp.concatenate([y_rw, y_gd, y5], axis=1)
    return x_ref[...] + mod_ref[2:3, :] * _mm(mix, wout_ref[...])


def _mix_ffn_kernel(hidden, final, n_mix, *refs):
    mix_refs, (g_ref, wgu_ref, wd_ref, fg_ref, o_ref) = refs[:n_mix], refs[n_mix:]
    mod_ref = mix_refs[1]
    x = _mix_residual(*mix_refs)
    ms = jnp.mean(x * x, axis=-1, keepdims=True)
    h = x * lax.rsqrt(ms + NORM_EPS) * g_ref[...]
    h = h * (1.0 + mod_ref[4:5, :]) + mod_ref[3:4, :]
    gu = _mm(h, wgu_ref[...])
    act = _silu(gu[:, :hidden]) * gu[:, hidden:]
    o = x + mod_ref[5:6, :] * _mm(act, wd_ref[...])
    if final:
        ms = jnp.mean(o * o, axis=-1, keepdims=True)
        o = o * lax.rsqrt(ms + NORM_EPS) * fg_ref[...]
    o_ref[...] = o


def _mix_ffn(xs, mod, mod_row, z_rw, toks, params, gain, w_gu, w_down, final_gain, final):
    bsz, n, d = xs.shape
    hidden = w_down.shape[0]
    tm = min(ROW_TILE, n)
    tok_spec = lambda a: pl.BlockSpec((None, tm, a.shape[-1]), lambda b, i: (b, i, 0))
    const = lambda a: pl.BlockSpec(a.shape, lambda b, i, nd=a.ndim: (0,) * nd,
                                   pipeline_mode=pl.Buffered(1))
    gate_in = z_rw.shape[-1] // RW_GATE_LORA - 1
    assert RW_GATE_LORA == LANES and z_rw.shape[-1] % LANES == 0
    mix_specs = ([tok_spec(xs), pl.BlockSpec((None, 6, d), lambda b, i: (mod_row(b), 0, 0)),
                  pl.BlockSpec((None, tm, LANES), lambda b, i: (b, i, gate_in))]
                 + [tok_spec(a) for a in toks] + [const(a) for a in params])
    return pl.pallas_call(
        functools.partial(_mix_ffn_kernel, hidden, final, len(mix_specs)),
        grid=(bsz, n // tm),
        in_specs=mix_specs + [const(gain), const(w_gu), const(w_down), const(final_gain)],
        out_specs=tok_spec(xs),
        out_shape=jax.ShapeDtypeStruct(xs.shape, F32),
        compiler_params=_cparams("parallel", "parallel"),
    )(xs, mod, z_rw, *toks, *params, gain, w_gu, w_down, final_gain)


def kernel(x, c, ctx, c_ctx, norm1_g, norm2_g, final_g, ada_w, ada_b, w_in, w_out,
           rw_mu, rw_w0, rw_wup, rw_a0, rw_aup, rw_gup, rw_kk, rw_ka, rw_rk, rw_lnx_g, rw_lnx_b,
           gd_conv, gd_a_log, gd_dt_bias, gd_norm_g,
           s5_lam_re, s5_lam_im, s5_log_dt, s5_b_re, s5_b_im, s5_c_re, s5_c_im, s5_d, s5_glu_w, s5_glu_b,
           ffn_w_gate, ffn_w_up, ffn_w_down):
    bsz, n, d = x.shape
    depth = w_in.shape[0]
    rw_w = rw_kk.shape[-1]
    gd_w = gd_conv.shape[-1] // 3
    s5_w = s5_d.shape[-1]
    rw_heads = rw_w // HEAD_DIM
    gd_heads = gd_w // HEAD_DIM
    in_rw = rw_mu.shape[-1]
    assert 2 * bsz == SUBLANES and n % GRID_W == 0 and n % CHUNK == 0 and ctx.shape[1] % CHUNK == 0
    assert rw_heads % 2 == 0 and gd_heads % 2 == 0 and 2 * HEAD_DIM == LANES

    cond = jnp.concatenate([c, c_ctx[None, :], jnp.zeros((SUBLANES - bsz - 1, d), F32)], axis=0)
    mod_all = _ada_modulation(cond, ada_w, ada_b).reshape(depth, SUBLANES, 6, d)
    x_row = lambda b: b
    ctx_row = lambda b: bsz

    o_gd = in_rw
    o_ba = o_gd + 3 * gd_w
    o_gate = o_ba + 4 * gd_heads
    o_s5 = o_gate + gd_w
    splits = (in_rw, 3 * gd_w, gd_w, s5_w, LANES)
    w_perm = jnp.concatenate(
        [w_in[:, :, :o_gd], w_in[:, :, o_gd:o_ba], w_in[:, :, o_gate:o_s5], w_in[:, :, o_s5:],
         w_in[:, :, o_ba:o_gate], jnp.zeros((depth, d, LANES - 4 * gd_heads), F32)], axis=-1).astype(MXU_DTYPE)
    w_gu = jnp.concatenate([ffn_w_gate, ffn_w_up], axis=-1).astype(MXU_DTYPE)
    w_dn = ffn_w_down.astype(MXU_DTYPE)
    w_o = w_out.astype(MXU_DTYPE)
    glu_w = s5_glu_w.astype(MXU_DTYPE)
    row = lambda t: t.reshape(1, -1)

    for l in range(depth):
        p = {'rw_mu': row(rw_mu[l]), 'rw_w0': rw_w0[l], 'rw_wup': rw_wup[l], 'rw_a0': rw_a0[l],
             'rw_aup': rw_aup[l], 'rw_gup': rw_gup[l], 'rw_kk': row(rw_kk[l]), 'rw_ka': row(rw_ka[l]),
             'rw_rk': row(rw_rk[l]), 'gd_conv': gd_conv[l], 'gd_a_log': gd_a_log[l],
             'gd_dt_bias': gd_dt_bias[l], 's5_lam_re': s5_lam_re[l], 's5_lam_im': s5_lam_im[l],
             's5_log_dt': s5_log_dt[l], 's5_b_re': s5_b_re[l], 's5_b_im': s5_b_im[l],
             's5_c_re': s5_c_re[l], 's5_c_im': s5_c_im[l]}
        mix_params = [rw_gup[l], row(rw_lnx_g[l]), row(rw_lnx_b[l]), row(jnp.tile(gd_norm_g[l], gd_heads)),
                      row(s5_d[l]), glu_w[l], row(s5_glu_b[l]), w_o[l]]
        disc = _s5_discretise(p)
        mod = mod_all[l]
        states = (jnp.zeros((2, bsz, rw_heads // 2, LANES, LANES), F32),
                  jnp.zeros((2, bsz, gd_heads // 2, LANES, LANES), F32),
                  (jnp.zeros((SUBLANES, disc[0].shape[1]), F32),) * 2)
        new_streams = []
        for stream, mod_row, column_major, is_ctx in ((ctx, ctx_row, False, True), (x, x_row, True, False)):
            z_rw, kk, z_q, z_g, z_s5, z_ba = _in_projection(stream, mod, mod_row, row(norm1_g[l]), w_perm[l], splits,
                                                            p['rw_mu'], p['rw_kk'], p['gd_conv'])
            yf, yb, bonf, bonb, s_rw = _rwkv_scan(z_rw, kk, p, states[0])
            of, ob, s_gd = _gdn_scan(z_q, z_ba, p, states[1])
            yf5, yb5, s_s5 = _s5_scan(_to_scan_order(z_s5, column_major), disc, states[2])
            s5f = _from_scan_order(yf5, column_major)
            s5b = _from_scan_order(yb5, column_major)
            states = (s_rw, s_gd, s_s5)
            if is_ctx and l == depth - 1:
                new_streams.append(stream)
                continue
            final = (not is_ctx) and l == depth - 1
            new_streams.append(_mix_ffn(stream, mod, mod_row, z_rw, [yf, yb, bonf, bonb, of, ob, z_g, s5f, s5b, z_s5],
                                        mix_params, row(norm2_g[l]), w_gu[l], w_dn[l], row(final_g), final))
        ctx, x = new_streams
    return x
```
